```python
import math
import jax, jax.numpy as jnp
from jax import lax
import numpy as np


D_MODEL = 1024
BATCH = 4
SEQ = 8192
DEPTH = 4
DEC_BATCH = 1
DEC_SEQ = 16384
PAST_LEN = 128

HEAD_DIM = 64
GRID_W = 64
NA_HEADS = 4
NA_ROWS = 8
NA_COLS = 16
SW_HEADS = 4
SW_KV_HEADS = 2
SW_WINDOW = 128
DIL_CONFIGS = ((128, 1), (512, 4), (2048, 16))
N_DIL = 3
DIL_HEADS = 4
DIFF_HEADS = 4
DIFF_QK_DIM = HEAD_DIM // 2
Q_BLOCK = 128
ROPE_THETA = 10000.0
N_EXPERTS = 16
EC_CAPACITY_FACTOR = 2
D_EXPERT = D_MODEL
PLE_DIM = 256

MIX_WIDTH = (NA_HEADS + SW_HEADS + DIL_HEADS + DIFF_HEADS) * HEAD_DIM
IN_SPLITS = (
    NA_HEADS * HEAD_DIM, NA_HEADS * HEAD_DIM, NA_HEADS * HEAD_DIM,
    SW_HEADS * HEAD_DIM, SW_KV_HEADS * HEAD_DIM, SW_KV_HEADS * HEAD_DIM,
    N_DIL * DIL_HEADS * HEAD_DIM, N_DIL * DIL_HEADS * HEAD_DIM, N_DIL * DIL_HEADS * HEAD_DIM,
    DIFF_HEADS * 2 * DIFF_QK_DIM, DIFF_HEADS * 2 * DIFF_QK_DIM, DIFF_HEADS * HEAD_DIM,
)
D_IN = sum(IN_SPLITS)
IN_OFFSETS = tuple(int(o) for o in np.cumsum(IN_SPLITS)[:-1])
NEG_INF = -1e30
NORM_EPS = 1e-6

kernel_name = 'hybrid_parallel_heads_encoder'


def rmsnorm(x, g):
    xf = x.astype(jnp.float32)
    y = xf * lax.rsqrt(jnp.mean(xf * xf, axis=-1, keepdims=True) + NORM_EPS)
    return (y * g.astype(jnp.float32)).astype(x.dtype)


def rope(x, pos):
    half = x.shape[-1] // 2
    inv_freq = ROPE_THETA ** (-jnp.arange(half, dtype=jnp.float32) / half)
    ang = pos[:, None] * inv_freq[None, :]
    cos = jnp.cos(ang)[:, None, :]
    sin = jnp.sin(ang)[:, None, :]
    xf = x.astype(jnp.float32)
    x1, x2 = xf[..., :half], xf[..., half:]
    return jnp.concatenate([x1 * cos - x2 * sin, x2 * cos + x1 * sin], axis=-1).astype(x.dtype)


def banded_attention(q, k, v, window, sink=None):
    N, L, H, dh = q.shape
    G = k.shape[2]
    R = H // G
    blk = window
    nb = -(-L // blk)
    Lp = nb * blk
    K = blk + 2 * window
    qb = jnp.pad(q, ((0, 0), (0, Lp - L), (0, 0), (0, 0))).reshape(N, nb, blk, G, R, dh)
    pad_kv = ((0, 0), (window, Lp - L + window), (0, 0), (0, 0))
    kidx = jnp.arange(nb)[:, None] * blk + jnp.arange(K)[None, :]
    kb = jnp.pad(k, pad_kv)[:, kidx]
    vb = jnp.pad(v, pad_kv)[:, kidx]
    s = jnp.einsum('nbqgrd,nbkgd->nbgrqk', qb, kb).astype(jnp.float32) * (dh ** -0.5)
    qpos = jnp.arange(Lp).reshape(nb, blk)
    kpos = kidx - window
    valid = ((jnp.abs(qpos[:, :, None] - kpos[:, None, :]) <= window)
             & (kpos[:, None, :] >= 0) & (kpos[:, None, :] < L))
    s = jnp.where(valid[None, :, None, None], s, NEG_INF)
    m = jnp.max(s, axis=-1, keepdims=True)
    if sink is not None:
        sk = sink.astype(jnp.float32).reshape(G, R)[None, None, :, :, None, None]
        m = jnp.maximum(m, sk)
    e = jnp.exp(s - m)
    den = jnp.sum(e, axis=-1, keepdims=True)
    if sink is not None:
        den = den + jnp.exp(sk - m)
    pr = (e / den).astype(v.dtype)
    o = jnp.einsum('nbgrqk,nbkgd->nbqgrd', pr, vb).reshape(N, Lp, H, dh)[:, :L]
    lse = (m + jnp.log(den))[..., 0]
    lse = lse.transpose(0, 1, 4, 2, 3).reshape(N, Lp, H)[:, :L]
    return o, lse


def to_strided(x, d):
    N, T = x.shape[:2]
    rest = x.shape[2:]
    x = jnp.moveaxis(x.reshape((N, T // d, d) + rest), 2, 1)
    return x.reshape((N * d, T // d) + rest)


def from_strided(x, d, N):
    L = x.shape[1]
    rest = x.shape[2:]
    x = jnp.moveaxis(x.reshape((N, d, L) + rest), 1, 2)
    return x.reshape((N, L * d) + rest)


def neighborhood_attention(q, k, v, rpb):
    N, T, H, dh = q.shape
    rows = T // GRID_W
    kr = min(NA_ROWS, rows)
    kc = NA_COLS
    qg = q.reshape(N, rows, GRID_W, H, dh)
    kg = k.reshape(N, rows, GRID_W, H, dh)
    vg = v.reshape(N, rows, GRID_W, H, dh)
    cols = jnp.arange(GRID_W)
    cstart = jnp.clip(cols - kc // 2, 0, GRID_W - kc)
    cidx = cstart[:, None] + jnp.arange(kc)[None, :]
    dc = cidx - cols[:, None]

    def row(r):
        rstart = jnp.clip(r - kr // 2, 0, rows - kr)
        dr = rstart + jnp.arange(kr) - r
        kband = lax.dynamic_slice_in_dim(kg, rstart, kr, axis=1)
        vband = lax.dynamic_slice_in_dim(vg, rstart, kr, axis=1)
        kw = kband[:, :, cidx].transpose(0, 2, 1, 3, 4, 5).reshape(N, GRID_W, kr * kc, H, dh)
        vw = vband[:, :, cidx].transpose(0, 2, 1, 3, 4, 5).reshape(N, GRID_W, kr * kc, H, dh)
        qr = lax.dynamic_index_in_dim(qg, r, axis=1, keepdims=False)
        s = jnp.einsum('nchd,nckhd->nhck', qr, kw).astype(jnp.float32) * (dh ** -0.5)
        bias = rpb[:, dr[:, None, None] + NA_ROWS - 1, dc[None, :, :] + NA_COLS - 1]
        bias = bias.transpose(0, 2, 1, 3).reshape(H, GRID_W, kr * kc).astype(jnp.float32)
        p = jax.nn.softmax(s + bias[None], axis=-1).astype(v.dtype)
        return jnp.einsum('nhck,nckhd->nchd', p, vw)

    o = lax.map(row, jnp.arange(rows))
    return o.transpose(1, 0, 2, 3, 4).reshape(N, T, H, dh)


def diff_attention(q, k, v, lam):
    N, T, H, _, dq = q.shape
    nb = T // Q_BLOCK
    qb = q.reshape(N, nb, Q_BLOCK, H, 2, dq).transpose(1, 0, 2, 3, 4, 5)

    def block(qi):
        s = jnp.einsum('nqhmd,nkhmd->nhmqk', qi, k).astype(jnp.float32) * (dq ** -0.5)
        a = jax.nn.softmax(s, axis=-1)
        w = (a[:, :, 0] - lam * a[:, :, 1]).astype(v.dtype)
        return jnp.einsum('nhqk,nkhd->nqhd', w, v)

    o = lax.map(block, qb)
    return o.transpose(1, 0, 2, 3, 4).reshape(N, T, H, v.shape[-1])


def expert_choice_ffn(xn, w_router, w_gate, w_up, w_down):
    N, T, D = xn.shape
    n = N * T
    cap = EC_CAPACITY_FACTOR * n // N_EXPERTS
    xf = xn.reshape(n, D)
    aff = jax.nn.softmax((xf @ w_router).astype(jnp.float32), axis=-1)
    g, idx = lax.top_k(aff.T, cap)
    xe = xf[idx]
    hdn = jax.nn.silu(jnp.einsum('ecd,edf->ecf', xe, w_gate)) * jnp.einsum('ecd,edf->ecf', xe, w_up)
    ye = jnp.einsum('ecf,efd->ecd', hdn, w_down) * g[..., None].astype(xf.dtype)
    y = jnp.zeros_like(xf).at[idx.reshape(-1)].add(ye.reshape(-1, D))
    return y.reshape(N, T, D)


def _layer(h, p_i, layer_idx, attn_norm, w_in, rpb_a, qk_norm_a, qk_norm_b, sink_b, qk_norm_c,
           qk_norm_d, lambda_d, subln_d, w_out, ffn_norm, w_router, w_gate, w_up, w_down,
           ple_norm, w_ple_gate, w_ple_proj):
    N, T, _ = h.shape
    pos = jnp.arange(T, dtype=jnp.float32)
    xn = rmsnorm(h, attn_norm)
    qa, ka, va, qb, kb, vb, qc, kc, vc, qd, kd, vd = jnp.split(xn @ w_in, IN_OFFSETS, axis=-1)

    def heads(t, nh, dh=HEAD_DIM):
        return t.reshape(N, T, nh, dh)

    oa = neighborhood_attention(rmsnorm(heads(qa, NA_HEADS), qk_norm_a[0]),
                                rmsnorm(heads(ka, NA_HEADS), qk_norm_a[1]),
                                heads(va, NA_HEADS), rpb_a)

    qb_ = rope(rmsnorm(heads(qb, SW_HEADS), qk_norm_b[0]), pos)
    kb_ = rope(rmsnorm(heads(kb, SW_KV_HEADS), qk_norm_b[1]), pos)
    ob, _ = banded_attention(qb_, kb_, heads(vb, SW_KV_HEADS), SW_WINDOW, sink_b)

    qc_ = rope(rmsnorm(heads(qc, N_DIL * DIL_HEADS), qk_norm_c[0]), pos)
    kc_ = rope(rmsnorm(heads(kc, N_DIL * DIL_HEADS), qk_norm_c[1]), pos)
    vc_ = heads(vc, N_DIL * DIL_HEADS)
    outs, lses = [], []
    for g_i, (win, dil) in enumerate(DIL_CONFIGS):
        sl = slice(g_i * DIL_HEADS, (g_i + 1) * DIL_HEADS)
        o_g, l_g = banded_attention(to_strided(qc_[:, :, sl], dil), to_strided(kc_[:, :, sl], dil),
                                    to_strided(vc_[:, :, sl], dil), win // (2 * dil))
        outs.append(from_strided(o_g, dil, N))
        lses.append(from_strided(l_g, dil, N))
    wts = jax.nn.softmax(jnp.stack(lses), axis=0)
    oc = jnp.einsum('gnth,gnthd->nthd', wts.astype(h.dtype), jnp.stack(outs))

    qd_ = rope(rmsnorm(heads(qd, 2 * DIFF_HEADS, DIFF_QK_DIM), qk_norm_d[0]), pos).reshape(N, T, DIFF_HEADS, 2, DIFF_QK_DIM)
    kd_ = rope(rmsnorm(heads(kd, 2 * DIFF_HEADS, DIFF_QK_DIM), qk_norm_d[1]), pos).reshape(N, T, DIFF_HEADS, 2, DIFF_QK_DIM)
    lam_init = 0.8 - 0.6 * math.exp(-0.3 * layer_idx)
    lf = lambda_d.astype(jnp.float32)
    lam = jnp.exp(jnp.sum(lf[0] * lf[1])) - jnp.exp(jnp.sum(lf[2] * lf[3])) + lam_init
    od = diff_attention(qd_, kd_, heads(vd, DIFF_HEADS), lam)
    od = rmsnorm(od, subln_d) * (1.0 - lam_init)

    mix = jnp.concatenate([oa.reshape(N, T, -1), ob.reshape(N, T, -1),
                           oc.reshape(N, T, -1), od.reshape(N, T, -1)], axis=-1)
    h = h + mix @ w_out
    h = h + expert_choice_ffn(rmsnorm(h, ffn_norm), w_router, w_gate, w_up, w_down)
    gate = jax.nn.sigmoid(rmsnorm(h, ple_norm) @ w_ple_gate)
    return h + gate * (p_i @ w_ple_proj)


def _trunk(x, p, *params):
    h = x
    for i in range(DEPTH):
        h = _layer(h, p[i], i, *[w[i] for w in params])
    return h


def setup_inputs(seed: int = 0) -> dict:
    key = jax.random.key(seed)
    ks = jax.random.split(key, 23)
    f32 = jnp.float32

    def nrm(k, shape, scale=1.0):
        return jax.random.normal(k, shape, f32) * scale

    def gain(k, shape):
        return 1.0 + 0.05 * jax.random.normal(k, shape, f32)

    return {
        'x_prompt': nrm(ks[0], (BATCH, SEQ, D_MODEL)),
        'x_sample': nrm(ks[1], (DEC_BATCH, DEC_SEQ, D_MODEL)),
        'p_prompt': nrm(ks[2], (DEPTH, BATCH, SEQ, PLE_DIM)),
        'p_sample': nrm(ks[3], (DEPTH, DEC_BATCH, DEC_SEQ, PLE_DIM)),
        'attn_norm': gain(ks[4], (DEPTH, D_MODEL)),
        'w_in': nrm(ks[5], (DEPTH, D_MODEL, D_IN), D_MODEL ** -0.5),
        'rpb_a': nrm(ks[6], (DEPTH, NA_HEADS, 2 * NA_ROWS - 1, 2 * NA_COLS - 1), 0.1),
        'qk_norm_a': gain(ks[7], (DEPTH, 2, HEAD_DIM)),
        'qk_norm_b': gain(ks[8], (DEPTH, 2, HEAD_DIM)),
        'sink_b': nrm(ks[9], (DEPTH, SW_HEADS), 0.5),
        'qk_norm_c': gain(ks[10], (DEPTH, 2, HEAD_DIM)),
        'qk_norm_d': gain(ks[11], (DEPTH, 2, DIFF_QK_DIM)),
        'lambda_d': nrm(ks[12], (DEPTH, 4, DIFF_QK_DIM), 0.1),
        'subln_d': gain(ks[13], (DEPTH, HEAD_DIM)),
        'w_out': nrm(ks[14], (DEPTH, MIX_WIDTH, D_MODEL), MIX_WIDTH ** -0.5),
        'ffn_norm': gain(ks[15], (DEPTH, D_MODEL)),
        'w_router': nrm(ks[16], (DEPTH, D_MODEL, N_EXPERTS), D_MODEL ** -0.5),
        'w_gate': nrm(ks[17], (DEPTH, N_EXPERTS, D_MODEL, D_EXPERT), D_MODEL ** -0.5),
        'w_up': nrm(ks[18], (DEPTH, N_EXPERTS, D_MODEL, D_EXPERT), D_MODEL ** -0.5),
        'w_down': nrm(ks[19], (DEPTH, N_EXPERTS, D_EXPERT, D_MODEL), D_EXPERT ** -0.5),
        'ple_norm': gain(ks[20], (DEPTH, D_MODEL)),
        'w_ple_gate': nrm(ks[21], (DEPTH, D_MODEL, D_MODEL), D_MODEL ** -0.5),
        'w_ple_proj': nrm(ks[22], (DEPTH, PLE_DIM, D_MODEL), PLE_DIM ** -0.5),
    }


def reference(x_prompt, x_sample, p_prompt, p_sample, attn_norm, w_in, rpb_a, qk_norm_a, qk_norm_b,
              sink_b, qk_norm_c, qk_norm_d, lambda_d, subln_d, w_out, ffn_norm, w_router, w_gate,
              w_up, w_down, ple_norm, w_ple_gate, w_ple_proj):
    params = (attn_norm, w_in, rpb_a, qk_norm_a, qk_norm_b, sink_b, qk_norm_c, qk_norm_d, lambda_d,
              subln_d, w_out, ffn_norm, w_router, w_gate, w_up, w_down, ple_norm, w_ple_gate, w_ple_proj)
    y_prompt = _trunk(x_prompt, p_prompt, *params)
    y_sample = _trunk(x_sample, p_sample, *params)
    return (y_prompt, y_sample)
```

```python
import functools
import math

import jax
import jax.numpy as jnp
import numpy as np
from jax import lax
from jax.experimental import pallas as pl
from jax.experimental.pallas import tpu as pltpu

F32 = jnp.float32
BF16 = jnp.bfloat16
I32 = jnp.int32

D_MODEL = 1024
HEAD_DIM = 64
GRID_W = 64
NA_HEADS = 4
NA_ROWS = 8
NA_COLS = 16
SW_HEADS = 4
SW_KV_HEADS = 2
SW_WINDOW = 128
DIL_CONFIGS = ((128, 1), (512, 4), (2048, 16))
DIL_HEADS = 4
DIFF_HEADS = 4
DIFF_QK_DIM = HEAD_DIM // 2
ROPE_THETA = 10000.0
N_EXPERTS = 16
EC_CAPACITY_FACTOR = 2
PLE_DIM = 256
NEG_INF = -1e30
NORM_EPS = 1e-6

V7X_LANES = 128
V7X_VMEM_LIMIT = 56 * 1024 * 1024

CHUNK = 256
MAIN_COLS = 15 * CHUNK
N_IN_CHUNKS = 18
NA_QROWS = 2
NA_KROWS = NA_QROWS + NA_ROWS - 1
BAND_TQ = 128
DIFF_TQ = 128
DIFF_TK = 512
SLOT_BLK = 256
TOK_BLK = 256


def _cparams(n_axes):
    return pltpu.CompilerParams(
        dimension_semantics=("arbitrary",) * n_axes, vmem_limit_bytes=V7X_VMEM_LIMIT)


_IN_CHUNK_KIND = (
    (64, 0), (64, 0), (0, 0),
    (64, 32), (64, 32),
    (64, 32), (64, 32), (64, 32),
    (64, 32), (64, 32), (64, 32),
    (0, 0), (0, 0), (0, 0),
    (32, 16),
    (32, 16),
    (0, 0), (0, 0),
)
_KBVB_CHUNK = 4
_KD_CHUNK = 15


def _in_proj_kernel(h_ref, gn_ref, w_ref, b64_ref, b32_ref, gains_ref, cs64_ref, sn64_ref,
                    cs32_ref, sn32_ref, main_ref, kdt_ref, vdx_ref):
    x = h_ref[...]
    ms = jnp.mean(x * x, axis=-1, keepdims=True)
    xn = (x * lax.rsqrt(ms + NORM_EPS) * gn_ref[...]).astype(BF16)
    tm = x.shape[0]
    lane = lax.broadcasted_iota(I32, (tm, CHUNK), 1)
    for c in range(N_IN_CHUNKS):
        hd, half = _IN_CHUNK_KIND[c]
        y = jnp.dot(xn, w_ref[:, c * CHUNK:(c + 1) * CHUNK], preferred_element_type=F32)
        out = y
        if hd:
            bmat = b64_ref[...] if hd == 64 else b32_ref[...]
            msq = jnp.dot((y * y).astype(BF16), bmat, preferred_element_type=F32)
            out = y * lax.rsqrt(msq + NORM_EPS) * gains_ref[c:c + 1, :]
        if half:
            cs = cs64_ref[...] if half == 32 else cs32_ref[...]
            sn = sn64_ref[...] if half == 32 else sn32_ref[...]
            fwd = pltpu.roll(out, half, 1)
            bwd = pltpu.roll(out, CHUNK - half, 1)
            first = (lane & (2 * half - 1)) < half
            out = out * cs + jnp.where(first, bwd, fwd) * sn
        if c == _KBVB_CHUNK:
            out = jnp.where(lane < CHUNK // 2, out, y)
        if c < 15:
            main_ref[:, c * CHUNK:(c + 1) * CHUNK] = out.astype(BF16)
        elif c == _KD_CHUNK:
            kdt_ref[0] = out.T.astype(BF16)
        else:
            ones = jnp.where((lane & (V7X_LANES - 1)) >= HEAD_DIM, 1.0, 0.0)
            vdx_ref[:, (c - 16) * CHUNK:(c - 15) * CHUNK] = (out + ones).astype(BF16)


def _in_proj(h2, gn, w_ext, b64, b32, gains, tabs, n_seq, seq_len, tm):
    n = h2.shape[0]
    nblk = seq_len // tm
    cs64, sn64, cs32, sn32 = tabs
    const = lambda i: (0, 0)
    tab = lambda i: (i % nblk, 0)
    return pl.pallas_call(
        _in_proj_kernel,
        grid=(n // tm,),
        in_specs=[
            pl.BlockSpec((tm, D_MODEL), lambda i: (i, 0)),
            pl.BlockSpec((1, D_MODEL), const),
            pl.BlockSpec((D_MODEL, N_IN_CHUNKS * CHUNK), const),
            pl.BlockSpec((CHUNK, CHUNK), const),
            pl.BlockSpec((CHUNK, CHUNK), const),
            pl.BlockSpec((N_IN_CHUNKS, CHUNK), const),
            pl.BlockSpec((tm, CHUNK), tab),
            pl.BlockSpec((tm, CHUNK), tab),
            pl.BlockSpec((tm, CHUNK), tab),
            pl.BlockSpec((tm, CHUNK), tab),
        ],
        out_specs=[
            pl.BlockSpec((tm, MAIN_COLS), lambda i: (i, 0)),
            pl.BlockSpec((1, CHUNK, tm), lambda i: (i // nblk, 0, i % nblk)),
            pl.BlockSpec((tm, 2 * CHUNK), lambda i: (i, 0)),
        ],
        out_shape=[
            jax.ShapeDtypeStruct((n, MAIN_COLS), BF16),
            jax.ShapeDtypeStruct((n_seq, CHUNK, seq_len), BF16),
            jax.ShapeDtypeStruct((n, 2 * CHUNK), BF16),
        ],
        compiler_params=_cparams(1),
        name="in_proj",
    )(h2, gn, w_ext, b64, b32, gains, cs64, sn64, cs32, sn32)


def _rope_tables(seq_len, half, reps):
    inv_freq = ROPE_THETA ** (-jnp.arange(half, dtype=F32) / half)
    ang = jnp.arange(seq_len, dtype=F32)[:, None] * inv_freq[None, :]
    cos, sin = jnp.cos(ang), jnp.sin(ang)
    cs = jnp.tile(jnp.concatenate([cos, cos], axis=-1), (1, reps))
    sn = jnp.tile(jnp.concatenate([-sin, sin], axis=-1), (1, reps))
    return cs, sn


def _block_mean_matrix(head):
    blocks = CHUNK // head
    return jnp.asarray(np.kron(np.eye(blocks), np.full((head, head), 1.0 / head)), BF16)


def _extend_w_in(w_in):
    vd = w_in[:, 4096:4352].reshape(D_MODEL, DIFF_HEADS, HEAD_DIM)
    vdx = jnp.concatenate([vd, jnp.zeros_like(vd)], axis=-1).reshape(D_MODEL, 2 * CHUNK)
    return jnp.concatenate([w_in[:, :4096], vdx], axis=-1).astype(BF16)


def _in_gains(qk_a, qk_b, qk_c, qk_d):
    one = jnp.ones((CHUNK,), F32)
    s64 = HEAD_DIM ** -0.5
    s32 = DIFF_QK_DIM ** -0.5
    t4 = lambda g: jnp.tile(g.astype(F32), 4)
    rows = [
        t4(qk_a[0]) * s64, t4(qk_a[1]), one,
        t4(qk_b[0]) * s64, jnp.concatenate([jnp.tile(qk_b[1].astype(F32), 2), jnp.ones((128,), F32)]),
        t4(qk_c[0]) * s64, t4(qk_c[0]) * s64, t4(qk_c[0]) * s64,
        t4(qk_c[1]), t4(qk_c[1]), t4(qk_c[1]),
        one, one, one,
        jnp.tile(qk_d[0].astype(F32), 8) * s32,
        jnp.tile(qk_d[1].astype(F32), 8),
        one, one,
    ]
    return jnp.stack(rows)


def _na_kernel(kr0_ref, case_ref, q_ref, k_ref, v_ref, bias_ref, o_ref):
    del case_ref
    i = pl.program_id(1)
    ks = pl.multiple_of(kr0_ref[i] * GRID_W, GRID_W)
    nk = NA_KROWS * GRID_W
    q = q_ref[0]
    k = k_ref[0, pl.ds(ks, nk), :]
    v = v_ref[0, pl.ds(ks, nk), :]
    outs = []
    for h in range(NA_HEADS):
        sl = slice(h * HEAD_DIM, (h + 1) * HEAD_DIM)
        s = lax.dot_general(q[:, sl], k[:, sl], (((1,), (1,)), ((), ())),
                            preferred_element_type=F32)
        s = s + bias_ref[0, h]
        m = jnp.max(s, axis=-1, keepdims=True)
        e = jnp.exp(s - m)
        den = jnp.sum(e, axis=-1, keepdims=True)
        o = jnp.dot(e.astype(BF16), v[:, sl], preferred_element_type=F32)
        outs.append(o / den)
    o_ref[0] = jnp.concatenate(outs, axis=-1).astype(BF16)


def _na_plan(rows):
    kr, kc = NA_ROWS, NA_COLS
    nblk = rows // NA_QROWS
    kr0s, case_ids, cases = [], [], {}
    tables = []
    qi = np.arange(NA_QROWS * GRID_W)
    ki = np.arange(NA_KROWS * GRID_W)
    qr_l, qc = qi // GRID_W, qi % GRID_W
    kr_l, kcol = ki // GRID_W, ki % GRID_W
    cstart = np.clip(qc - kc // 2, 0, GRID_W - kc)
    for b in range(nblk):
        r0 = b * NA_QROWS
        kr0 = int(np.clip(r0 - kr // 2, 0, rows - NA_KROWS))
        rstart = np.clip(r0 + np.arange(NA_QROWS) - kr // 2, 0, rows - kr)
        key = (r0 - kr0, tuple(int(t) for t in rstart - kr0))
        if key not in cases:
            cases[key] = len(cases)
            q_row = r0 + qr_l
            k_row = kr0 + kr_l
            rs = rstart[qr_l]
            in_row = (k_row[None, :] >= rs[:, None]) & (k_row[None, :] < rs[:, None] + kr)
            in_col = (kcol[None, :] >= cstart[:, None]) & (kcol[None, :] < cstart[:, None] + kc)
            ir = np.clip(k_row[None, :] - q_row[:, None] + NA_ROWS - 1, 0, 2 * NA_ROWS - 2)
            ic = np.clip(kcol[None, :] - qc[:, None] + NA_COLS - 1, 0, 2 * NA_COLS - 2)
            tables.append((in_row & in_col, ir, ic))
        kr0s.append(kr0)
        case_ids.append(cases[key])
    mask = np.stack([t[0] for t in tables])
    ir = np.stack([t[1] for t in tables])
    ic = np.stack([t[2] for t in tables])
    return np.asarray(kr0s, np.int32), np.asarray(case_ids, np.int32), mask, ir, ic


def _na_bias(rpb, mask, ir, ic):
    b = rpb.astype(F32)[:, ir, ic]
    b = jnp.where(mask[None], b, NEG_INF)
    return jnp.transpose(b, (1, 0, 2, 3))


def _na_attention(main3, bias, kr0s, case_ids):
    n_seq, seq_len, _ = main3.shape
    tq = NA_QROWS * GRID_W
    nk = NA_KROWS * GRID_W
    grid = (n_seq, seq_len // tq)
    return pl.pallas_call(
        _na_kernel,
        grid_spec=pltpu.PrefetchScalarGridSpec(
            num_scalar_prefetch=2,
            grid=grid,
            in_specs=[
                pl.BlockSpec((1, tq, CHUNK), lambda s, i, kr, cs: (s, i, 0)),
                pl.BlockSpec((1, seq_len, CHUNK), lambda s, i, kr, cs: (s, 0, 1)),
                pl.BlockSpec((1, seq_len, CHUNK), lambda s, i, kr, cs: (s, 0, 2)),
                pl.BlockSpec((1, NA_HEADS, tq, nk), lambda s, i, kr, cs: (cs[i], 0, 0, 0)),
            ],
            out_specs=pl.BlockSpec((1, tq, CHUNK), lambda s, i, kr, cs: (s, i, 0)),
        ),
        out_shape=jax.ShapeDtypeStruct((n_seq, seq_len, CHUNK), BF16),
        compiler_params=_cparams(2),
        name="mixer_a",
    )(jnp.asarray(kr0s), jnp.asarray(case_ids), main3, main3, main3, bias)


def _band_kernel(*refs, window, n_kv, rep, length, has_sink, want_lse):
    refs = list(refs)
    sink_ref = refs.pop(0) if has_sink else None
    q_ref, k_ref, v_ref = refs[:3]
    o_ref = refs[3]
    lse_ref = refs[4] if want_lse else None
    tq = q_ref.shape[1]
    kw = tq + 2 * window
    i = pl.program_id(1)
    q0 = i * tq
    ks = pl.multiple_of(jnp.clip(q0 - window, 0, length - kw), 64)
    q = q_ref[0]
    k = k_ref[0, pl.ds(ks, kw), :]
    v = v_ref[0, pl.ds(ks, kw), :]
    qpos = q0 + lax.broadcasted_iota(I32, (tq, kw), 0)
    kpos = ks + lax.broadcasted_iota(I32, (tq, kw), 1)
    valid = jnp.abs(qpos - kpos) <= window
    outs, lses = [], []
    for h in range(n_kv * rep):
        g = h // rep
        qs = slice(h * HEAD_DIM, (h + 1) * HEAD_DIM)
        kvs = slice(g * HEAD_DIM, (g + 1) * HEAD_DIM)
        s = lax.dot_general(q[:, qs], k[:, kvs], (((1,), (1,)), ((), ())),
                            preferred_element_type=F32)
        s = jnp.where(valid, s, NEG_INF)
        m = jnp.max(s, axis=-1, keepdims=True)
        if has_sink:
            m = jnp.maximum(m, sink_ref[h])
        e = jnp.exp(s - m)
        den = jnp.sum(e, axis=-1, keepdims=True)
        if has_sink:
            den = den + jnp.exp(sink_ref[h] - m)
        o = jnp.dot(e.astype(BF16), v[:, kvs], preferred_element_type=F32)
        outs.append(o / den)
        if want_lse:
            lses.append(jnp.broadcast_to(m + jnp.log(den), (tq, HEAD_DIM)))
    o_ref[0] = jnp.concatenate(outs, axis=-1).astype(BF16)
    if want_lse:
        lse_ref[0] = jnp.concatenate(lses, axis=-1)


def _band_attention(src3, q_blk, k_blk, v_blk, kv_width, dil, n_cols_blk, window, n_kv, rep,
                    sink=None, want_lse=False):
    n_seq, length, _ = src3.shape
    tq = BAND_TQ
    qw = n_kv * rep * HEAD_DIM
    n_q_blk, n_kv_blk = n_cols_blk
    kern = functools.partial(_band_kernel, window=window, n_kv=n_kv, rep=rep, length=length,
                             has_sink=sink is not None, want_lse=want_lse)
    in_specs = [
        pl.BlockSpec((1, tq, qw), lambda s, i: (s // dil, i, (s % dil) * n_q_blk + q_blk)),
        pl.BlockSpec((1, length, kv_width), lambda s, i: (s // dil, 0, (s % dil) * n_kv_blk + k_blk)),
        pl.BlockSpec((1, length, kv_width), lambda s, i: (s // dil, 0, (s % dil) * n_kv_blk + v_blk)),
    ]
    args = [src3, src3, src3]
    if sink is not None:
        in_specs = [pl.BlockSpec(memory_space=pltpu.SMEM)] + in_specs
        args = [sink.astype(F32)] + args
    o_spec = pl.BlockSpec((1, tq, qw), lambda s, i: (s // dil, i, s % dil))
    out_specs = [o_spec]
    out_shape = [jax.ShapeDtypeStruct((n_seq, length, dil * qw), BF16)]
    if want_lse:
        out_specs.append(o_spec)
        out_shape.append(jax.ShapeDtypeStruct((n_seq, length, dil * qw), F32))
    res = pl.pallas_call(
        kern,
        grid=(n_seq * dil, length // tq),
        in_specs=in_specs,
        out_specs=out_specs,
        out_shape=out_shape,
        compiler_params=_cparams(2),
        name="mixer_band",
    )(*args)
    return res


def _diff_kernel(lam_ref, q_ref, kt_ref, vx_ref, g_ref, o_ref, *, seq_len):
    tq = q_ref.shape[1]
    tk = min(DIFF_TK, seq_len)
    q = q_ref[0]
    lam = lam_ref[0]
    post = lam_ref[1]
    outs = []
    for h in range(DIFF_HEADS):
        parts = []
        for m_i in range(2):
            j = 2 * h + m_i
            qj = q[:, j * DIFF_QK_DIM:(j + 1) * DIFF_QK_DIM]

            def body(c, carry, j=j, h=h, qj=qj):
                mi, acc = carry
                c0 = pl.multiple_of(c * tk, tk)
                kt = kt_ref[0, j * DIFF_QK_DIM:(j + 1) * DIFF_QK_DIM, pl.ds(c0, tk)]
                s = jnp.dot(qj, kt, preferred_element_type=F32)
                mn = jnp.maximum(mi, jnp.max(s, axis=-1, keepdims=True))
                p = jnp.exp(s - mn)
                alpha = jnp.exp(mi - mn)
                vv = vx_ref[0, pl.ds(c0, tk), h * V7X_LANES:(h + 1) * V7X_LANES]
                acc = alpha * acc + jnp.dot(p.astype(BF16), vv, preferred_element_type=F32)
                return mn, acc

            init = (jnp.full((tq, 1), NEG_INF, F32), jnp.zeros((tq, V7X_LANES), F32))
            _, acc = lax.fori_loop(0, seq_len // tk, body, init)
            parts.append(acc[:, :HEAD_DIM] / acc[:, HEAD_DIM:])
        od = parts[0] - lam * parts[1]
        ms = jnp.mean(od * od, axis=-1, keepdims=True)
        od = od * lax.rsqrt(ms + NORM_EPS) * g_ref[...] * post
        outs.append(od)
    o_ref[0] = jnp.concatenate(outs, axis=-1).astype(BF16)


def _diff_attention(main3, kdt, vdx3, lam_post, subln):
    n_seq, seq_len, _ = main3.shape
    tq = DIFF_TQ
    kern = functools.partial(_diff_kernel, seq_len=seq_len)
    return pl.pallas_call(
        kern,
        grid=(n_seq, seq_len // tq),
        in_specs=[
            pl.BlockSpec(memory_space=pltpu.SMEM),
            pl.BlockSpec((1, tq, CHUNK), lambda s, i: (s, i, 14)),
            pl.BlockSpec((1, CHUNK, seq_len), lambda s, i: (s, 0, 0)),
            pl.BlockSpec((1, seq_len, 2 * CHUNK), lambda s, i: (s, 0, 0)),
            pl.BlockSpec((1, HEAD_DIM), lambda s, i: (0, 0)),
        ],
        out_specs=pl.BlockSpec((1, tq, CHUNK), lambda s, i: (s, i, 0)),
        out_shape=jax.ShapeDtypeStruct((n_seq, seq_len, CHUNK), BF16),
        compiler_params=_cparams(2),
        name="mixer_d",
    )(lam_post, main3, kdt, vdx3, subln)


def _out_proj_kernel(h_ref, oa_ref, ob_ref, oc0_ref, oc1_ref, oc2_ref, l0_ref, l1_ref, l2_ref,
                     od_ref, w_ref, fn_ref, wr_ref, h1_ref, hn_ref, aff_ref):
    l0, l1, l2 = l0_ref[...], l1_ref[...], l2_ref[...]
    mx = jnp.maximum(jnp.maximum(l0, l1), l2)
    e0, e1, e2 = jnp.exp(l0 - mx), jnp.exp(l1 - mx), jnp.exp(l2 - mx)
    den = e0 + e1 + e2
    oc = ((e0 / den) * oc0_ref[...].astype(F32) + (e1 / den) * oc1_ref[...].astype(F32)
          + (e2 / den) * oc2_ref[...].astype(F32))
    acc = h_ref[...]
    acc = acc + jnp.dot(oa_ref[...], w_ref[0 * CHUNK:1 * CHUNK, :], preferred_element_type=F32)
    acc = acc + jnp.dot(ob_ref[...], w_ref[1 * CHUNK:2 * CHUNK, :], preferred_element_type=F32)
    acc = acc + jnp.dot(oc.astype(BF16), w_ref[2 * CHUNK:3 * CHUNK, :], preferred_element_type=F32)
    acc = acc + jnp.dot(od_ref[...], w_ref[3 * CHUNK:4 * CHUNK, :], preferred_element_type=F32)
    h1_ref[...] = acc
    ms = jnp.mean(acc * acc, axis=-1, keepdims=True)
    hn = (acc * lax.rsqrt(ms + NORM_EPS) * fn_ref[...]).astype(BF16)
    hn_ref[...] = hn
    logits = jnp.dot(hn, wr_ref[...], preferred_element_type=F32)
    m = jnp.max(logits, axis=-1, keepdims=True)
    e = jnp.exp(logits - m)
    aff_ref[...] = e / jnp.sum(e, axis=-1, keepdims=True)


def _out_proj(h2, oa, ob, ocs, lses, od, w_out, ffn_norm, w_router, tm):
    n = h2.shape[0]
    row = lambda w: pl.BlockSpec((tm, w), lambda i: (i, 0))
    const = lambda r, c: pl.BlockSpec((r, c), lambda i: (0, 0))
    return pl.pallas_call(
        _out_proj_kernel,
        grid=(n // tm,),
        in_specs=[row(D_MODEL)] + [row(CHUNK)] * 9 + [
            const(D_MODEL, D_MODEL), const(1, D_MODEL), const(D_MODEL, N_EXPERTS)],
        out_specs=[row(D_MODEL), row(D_MODEL), row(N_EXPERTS)],
        out_shape=[
            jax.ShapeDtypeStruct((n, D_MODEL), F32),
            jax.ShapeDtypeStruct((n, D_MODEL), BF16),
            jax.ShapeDtypeStruct((n, N_EXPERTS), F32),
        ],
        compiler_params=_cparams(1),
        name="out_proj",
    )(h2, oa, ob, ocs[0], ocs[1], ocs[2], lses[0], lses[1], lses[2], od, w_out, ffn_norm,
      w_router)


def _select_kernel(aff_ref, posm_ref, pref_ref, *, cap, n_tok):
    def bits():
        return lax.bitcast_convert_type(aff_ref[...], I32)

    def bit_body(i, thr):
        cand = thr | jnp.left_shift(jnp.int32(1), 30 - i)
        cnt = jnp.sum(jnp.where(bits() >= cand, 1.0, 0.0), axis=-1, keepdims=True)
        return jnp.where(cnt >= cap, cand, thr)

    thr = lax.fori_loop(0, 31, bit_body, jnp.zeros((N_EXPERTS, 1), I32))
    cnt_gt = jnp.sum(jnp.where(bits() > thr, 1.0, 0.0), axis=-1, keepdims=True)
    need = cap - cnt_gt
    blk = SLOT_BLK
    r = lax.broadcasted_iota(I32, (blk, blk), 0)
    c = lax.broadcasted_iota(I32, (blk, blk), 1)
    upper = jnp.where(r < c, 1.0, 0.0).astype(BF16)

    def blk_body(b, carry):
        ceq, csel = carry
        b0 = pl.multiple_of(b * blk, blk)
        xb = lax.bitcast_convert_type(aff_ref[:, pl.ds(b0, blk)], I32)
        gt = xb > thr
        eq = xb == thr
        eqf = jnp.where(eq, 1.0, 0.0)
        rank = ceq + jnp.dot(eqf.astype(BF16), upper, preferred_element_type=F32)
        sel = gt | (eq & (rank < need))
        self_ = jnp.where(sel, 1.0, 0.0)
        pref = csel + jnp.dot(self_.astype(BF16), upper, preferred_element_type=F32)
        prefi = pref.astype(I32)
        pref_ref[:, pl.ds(b0, blk)] = prefi
        posm_ref[:, pl.ds(b0, blk)] = jnp.where(sel, prefi, -1)
        return (ceq + jnp.sum(eqf, axis=-1, keepdims=True),
                csel + jnp.sum(self_, axis=-1, keepdims=True))

    zero = jnp.zeros((N_EXPERTS, 1), F32)
    lax.fori_loop(0, n_tok // blk, blk_body, (zero, zero))


def _select(aff_t, cap):
    n_tok = aff_t.shape[1]
    kern = functools.partial(_select_kernel, cap=cap, n_tok=n_tok)
    full = pl.BlockSpec((N_EXPERTS, n_tok), lambda i: (0, 0))
    return pl.pallas_call(
        kern,
        grid=(1,),
        in_specs=[full],
        out_specs=[full, full],
        out_shape=[jax.ShapeDtypeStruct((N_EXPERTS, n_tok), I32)] * 2,
        compiler_params=_cparams(1),
        name="select",
    )(aff_t)


_VALID, _FIRST, _LAST = 1, 2, 4


def _route_plan(pref, cap):
    n_exp, n_tok = pref.shape
    nb = n_tok // TOK_BLK
    nw = cap // SLOT_BLK
    off = pref[:, ::TOK_BLK]
    end = jnp.concatenate([off[:, 1:], jnp.full((n_exp, 1), cap, I32)], axis=1)
    cnt = end - off
    w_lo = jnp.minimum(off // SLOT_BLK, nw - 1)
    w_hi = jnp.where(cnt > 0, (end - 1) // SLOT_BLK, w_lo)

    s_g = nb + nw
    nst = w_hi - w_lo + 1
    start = jnp.cumsum(nst, axis=1) - nst
    total = jnp.sum(nst, axis=1, keepdims=True)
    s_idx = jnp.arange(s_g, dtype=I32)[None, :]
    s_eff = jnp.minimum(s_idx, total - 1)
    b_s = jax.vmap(lambda st, se: jnp.searchsorted(st, se, side="right"))(start, s_eff) - 1
    b_s = b_s.astype(I32)
    w_s = jnp.take_along_axis(w_lo, b_s, axis=1) + (s_eff - jnp.take_along_axis(start, b_s, axis=1))
    valid = s_idx < total
    prev_w = jnp.concatenate([jnp.full((n_exp, 1), -1, I32), w_s[:, :-1]], axis=1)
    next_w = jnp.concatenate([w_s[:, 1:], jnp.full((n_exp, 1), -1, I32)], axis=1)
    first = valid & (w_s != prev_w)
    last = valid & ((w_s != next_w) | (s_idx == total - 1))
    g_flag = valid * _VALID + first * _FIRST + last * _LAST
    gather_plan = (b_s.reshape(-1), w_s.reshape(-1).astype(I32), g_flag.reshape(-1).astype(I32))

    s_c = n_exp * (nb + nw)
    cnt_t, wlo_t, whi_t = cnt.T, w_lo.T, w_hi.T
    nst_c = jnp.where(cnt_t > 0, whi_t - wlo_t + 1, 0)
    nst_c = nst_c.at[:, 0].set(jnp.maximum(nst_c[:, 0], 1)).reshape(-1)
    start_c = jnp.cumsum(nst_c) - nst_c
    total_c = jnp.sum(nst_c)
    c_idx = jnp.arange(s_c, dtype=I32)
    c_eff = jnp.minimum(c_idx, total_c - 1)
    pair = (jnp.searchsorted(start_c, c_eff, side="right") - 1).astype(I32)
    cb = pair // n_exp
    ce = pair % n_exp
    cw = wlo_t.reshape(-1)[pair] + (c_eff - start_c[pair])
    cvalid = c_idx < total_c
    prev_b = jnp.concatenate([jnp.full((1,), -1, I32), cb[:-1]])
    next_b = jnp.concatenate([cb[1:], jnp.full((1,), -1, I32)])
    cfirst = cvalid & (cb != prev_b)
    clast = cvalid & ((cb != next_b) | (c_idx == total_c - 1))
    c_flag = cvalid * _VALID + cfirst * _FIRST + clast * _LAST
    combine_plan = (cb, ce, cw.astype(I32), c_flag.astype(I32))
    return gather_plan, combine_plan, s_g, s_c


def _gather_kernel(gb_ref, gw_ref, gf_ref, posm_ref, g_ref, hn_ref, xe_ref, gs_ref, acc_ref,
                   gacc_ref, *, n_steps):
    e = pl.program_id(0)
    s = pl.program_id(1)
    flag = gf_ref[e * n_steps + s]
    w = gw_ref[e * n_steps + s]

    @pl.when((flag & _FIRST) != 0)
    def _():
        acc_ref[...] = jnp.zeros_like(acc_ref)
        gacc_ref[...] = jnp.zeros_like(gacc_ref)

    @pl.when((flag & _VALID) != 0)
    def _():
        rel = posm_ref[0] - w * SLOT_BLK
        hit = lax.broadcasted_iota(I32, (SLOT_BLK, TOK_BLK), 0) == rel
        onehot = jnp.where(hit, 1.0, 0.0).astype(BF16)
        acc_ref[...] += jnp.dot(onehot, hn_ref[...], preferred_element_type=F32)
        gacc_ref[...] += jnp.sum(jnp.where(hit, g_ref[0], 0.0), axis=-1, keepdims=True)

    @pl.when((flag & _LAST) != 0)
    def _():
        xe_ref[0] = acc_ref[...].astype(BF16)
        gs_ref[0] = gacc_ref[...]


def _gather(plan, s_g, posm3, g3, hn, cap):
    gb, gw, gf = plan
    n_tok = hn.shape[0]
    kern = functools.partial(_gather_kernel, n_steps=s_g)
    tok = lambda e, s, gb, gw, gf: (e, 0, gb[e * s_g + s])
    slot = lambda e, s, gb, gw, gf: (e, gw[e * s_g + s], 0)
    return pl.pallas_call(
        kern,
        grid_spec=pltpu.PrefetchScalarGridSpec(
            num_scalar_prefetch=3,
            grid=(N_EXPERTS, s_g),
            in_specs=[
                pl.BlockSpec((1, 1, TOK_BLK), tok),
                pl.BlockSpec((1, 1, TOK_BLK), tok),
                pl.BlockSpec((TOK_BLK, D_MODEL), lambda e, s, gb, gw, gf: (gb[e * s_g + s], 0)),
            ],
            out_specs=[
                pl.BlockSpec((1, SLOT_BLK, D_MODEL), slot),
                pl.BlockSpec((1, SLOT_BLK, 1), slot),
            ],
            scratch_shapes=[pltpu.VMEM((SLOT_BLK, D_MODEL), F32), pltpu.VMEM((SLOT_BLK, 1), F32)],
        ),
        out_shape=[
            jax.ShapeDtypeStruct((N_EXPERTS, cap, D_MODEL), BF16),
            jax.ShapeDtypeStruct((N_EXPERTS, cap, 1), F32),
        ],
        compiler_params=_cparams(2),
        name="gather",
    )(gb, gw, gf, posm3, g3, hn)


def _experts_kernel(x_ref, gs_ref, wg_ref, wu_ref, wd_ref, y_ref):
    x = x_ref[0]
    hg = jnp.dot(x, wg_ref[0], preferred_element_type=F32)
    hu = jnp.dot(x, wu_ref[0], preferred_element_type=F32)
    hdn = (hg / (1.0 + jnp.exp(-hg)) * hu).astype(BF16)
    y = jnp.dot(hdn, wd_ref[0], preferred_element_type=F32)
    y_ref[0] = (y * gs_ref[0]).astype(BF16)


def _experts(xe, gs, w_gate, w_up, w_down):
    n_exp, cap, _ = xe.shape
    tc = min(512, cap)
    wspec = pl.BlockSpec((1, D_MODEL, D_MODEL), lambda e, c: (e, 0, 0))
    return pl.pallas_call(
        _experts_kernel,
        grid=(n_exp, cap // tc),
        in_specs=[
            pl.BlockSpec((1, tc, D_MODEL), lambda e, c: (e, c, 0)),
            pl.BlockSpec((1, tc, 1), lambda e, c: (e, c, 0)),
            wspec, wspec, wspec,
        ],
        out_specs=pl.BlockSpec((1, tc, D_MODEL), lambda e, c: (e, c, 0)),
        out_shape=jax.ShapeDtypeStruct((n_exp, cap, D_MODEL), BF16),
        compiler_params=_cparams(2),
        name="experts",
    )(xe, gs, w_gate, w_up, w_down)


def _combine_kernel(cb_ref, ce_ref, cw_ref, cf_ref, h1_ref, posmt_ref, ye_ref, p_ref, pn_ref,
                    wpg_ref, wpp_ref, o_ref, acc_ref):
    del cb_ref
    s = pl.program_id(0)
    flag = cf_ref[s]
    e = ce_ref[s]
    w = cw_ref[s]

    @pl.when((flag & _FIRST) != 0)
    def _():
        acc_ref[...] = jnp.zeros_like(acc_ref)

    @pl.when((flag & _VALID) != 0)
    def _():
        pm = posmt_ref[...].astype(F32)
        lane = lax.broadcasted_iota(I32, pm.shape, 1)
        col = jnp.sum(jnp.where(lane == e, pm, 0.0), axis=-1, keepdims=True)
        rel = col.astype(I32) - w * SLOT_BLK
        hit = lax.broadcasted_iota(I32, (TOK_BLK, SLOT_BLK), 1) == rel
        onehot = jnp.where(hit, 1.0, 0.0).astype(BF16)
        acc_ref[...] += jnp.dot(onehot, ye_ref[0], preferred_element_type=F32)

    @pl.when((flag & _LAST) != 0)
    def _():
        h2 = h1_ref[...] + acc_ref[...]
        ms = jnp.mean(h2 * h2, axis=-1, keepdims=True)
        hn = (h2 * lax.rsqrt(ms + NORM_EPS) * pn_ref[...]).astype(BF16)
        z = jnp.dot(hn, wpg_ref[...], preferred_element_type=F32)
        gate = 1.0 / (1.0 + jnp.exp(-z))
        proj = jnp.dot(p_ref[...].astype(BF16), wpp_ref[...], preferred_element_type=F32)
        o_ref[...] = h2 + gate * proj


def _combine(plan, s_c, h1, posm_t, ye, p2, ple_norm, w_ple_gate, w_ple_proj, cap):
    cb, ce, cw, cf = plan
    n_tok = h1.shape[0]
    nw = cap // SLOT_BLK
    ye3 = ye.reshape(N_EXPERTS * nw, SLOT_BLK, D_MODEL)
    tokb = lambda width: pl.BlockSpec((TOK_BLK, width), lambda s, cb, ce, cw, cf: (cb[s], 0))
    const = lambda r, c: pl.BlockSpec((r, c), lambda s, cb, ce, cw, cf: (0, 0))
    return pl.pallas_call(
        _combine_kernel,
        grid_spec=pltpu.PrefetchScalarGridSpec(
            num_scalar_prefetch=4,
            grid=(s_c,),
            in_specs=[
                tokb(D_MODEL),
                tokb(N_EXPERTS),
                pl.BlockSpec((1, SLOT_BLK, D_MODEL),
                             lambda s, cb, ce, cw, cf: (ce[s] * nw + cw[s], 0, 0)),
                tokb(PLE_DIM),
                const(1, D_MODEL), const(D_MODEL, D_MODEL), const(PLE_DIM, D_MODEL),
            ],
            out_specs=tokb(D_MODEL),
            scratch_shapes=[pltpu.VMEM((TOK_BLK, D_MODEL), F32)],
        ),
        out_shape=jax.ShapeDtypeStruct((n_tok, D_MODEL), F32),
        compiler_params=_cparams(1),
        name="combine",
    )(cb, ce, cw, cf, h1, posm_t, ye3, p2, ple_norm, w_ple_gate, w_ple_proj)


def _layer(h2, p2, lw, static):
    n_seq, seq_len, tabs, b64, b32, na_plan = static
    n_tok = h2.shape[0]
    tm = min(512, seq_len)
    gains = _in_gains(lw["qk_norm_a"], lw["qk_norm_b"], lw["qk_norm_c"], lw["qk_norm_d"])
    main, kdt, vdx = _in_proj(h2, lw["attn_norm"][None, :], lw["w_in"], b64, b32, gains, tabs,
                              n_seq, seq_len, tm)
    main3 = main.reshape(n_seq, seq_len, MAIN_COLS)

    kr0s, case_ids, mask, ir, ic = na_plan
    oa = _na_attention(main3, _na_bias(lw["rpb_a"], mask, ir, ic), kr0s, case_ids)

    (ob,) = _band_attention(main3, 3, 8, 9, V7X_LANES, 1, (MAIN_COLS // CHUNK, MAIN_COLS // V7X_LANES),
                            SW_WINDOW, SW_KV_HEADS, SW_HEADS // SW_KV_HEADS, sink=lw["sink_b"])
    ocs, lses = [], []
    for g_i, (win, dil) in enumerate(DIL_CONFIGS):
        src = main.reshape(n_seq, seq_len // dil, dil * MAIN_COLS)
        nblk = MAIN_COLS // CHUNK
        o_g, l_g = _band_attention(src, 5 + g_i, 8 + g_i, 11 + g_i, CHUNK, dil, (nblk, nblk),
                                   win // (2 * dil), DIL_HEADS, 1, want_lse=True)
        ocs.append(o_g.reshape(n_tok, CHUNK))
        lses.append(l_g.reshape(n_tok, CHUNK))

    lf = lw["lambda_d"].astype(F32)
    lam = jnp.exp(jnp.sum(lf[0] * lf[1])) - jnp.exp(jnp.sum(lf[2] * lf[3])) + lw["lam_init"]
    lam_post = jnp.stack([lam, 1.0 - lw["lam_init"]]).astype(F32)
    od = _diff_attention(main3, kdt, vdx.reshape(n_seq, seq_len, 2 * CHUNK), lam_post,
                         lw["subln_d"].astype(F32)[None, :])

    h1, hn, aff = _out_proj(h2, oa.reshape(n_tok, CHUNK), ob.reshape(n_tok, CHUNK), ocs, lses,
                            od.reshape(n_tok, CHUNK), lw["w_out"], lw["ffn_norm"][None, :],
                            lw["w_router"], tm)

    cap = EC_CAPACITY_FACTOR * n_tok // N_EXPERTS
    aff_t = aff.T
    posm, pref = _select(aff_t, cap)
    gather_plan, combine_plan, s_g, s_c = _route_plan(pref, cap)
    xe, gs = _gather(gather_plan, s_g, posm.reshape(N_EXPERTS, 1, n_tok),
                     aff_t.reshape(N_EXPERTS, 1, n_tok), hn, cap)
    ye = _experts(xe, gs, lw["w_gate"], lw["w_up"], lw["w_down"])
    return _combine(combine_plan, s_c, h1, posm.T, ye, p2, lw["ple_norm"][None, :],
                    lw["w_ple_gate"], lw["w_ple_proj"], cap)


def _trunk(x, p, weights):
    n_seq, seq_len, _ = x.shape
    depth = p.shape[0]
    tabs = _rope_tables(seq_len, HEAD_DIM // 2, 4) + _rope_tables(seq_len, DIFF_QK_DIM // 2, 8)
    static = (n_seq, seq_len, tabs, _block_mean_matrix(HEAD_DIM), _block_mean_matrix(DIFF_QK_DIM),
              _na_plan(seq_len // GRID_W))
    xs = dict(weights)
    xs["p"] = p.reshape(depth, n_seq * seq_len, PLE_DIM)

    def step(h2, lw):
        return _layer(h2, lw["p"], lw, static), None

    h2, _ = lax.scan(step, x.reshape(n_seq * seq_len, D_MODEL), xs)
    return h2.reshape(n_seq, seq_len, D_MODEL)


def kernel(x_prompt, x_sample, p_prompt, p_sample, attn_norm, w_in, rpb_a, qk_norm_a, qk_norm_b, sink_b, qk_norm_c, qk_norm_d, lambda_d, subln_d, w_out, ffn_norm, w_router, w_gate, w_up, w_down, ple_norm, w_ple_gate, w_ple_proj):
    depth = w_in.shape[0]
    lam_init = jnp.asarray([0.8 - 0.6 * math.exp(-0.3 * i) for i in range(depth)], F32)
    weights = dict(
        attn_norm=attn_norm.astype(F32), w_in=jax.vmap(_extend_w_in)(w_in), rpb_a=rpb_a,
        qk_norm_a=qk_norm_a, qk_norm_b=qk_norm_b, sink_b=sink_b, qk_norm_c=qk_norm_c,
        qk_norm_d=qk_norm_d, lambda_d=lambda_d, subln_d=subln_d, w_out=w_out.astype(BF16),
        ffn_norm=ffn_norm.astype(F32), w_router=w_router.astype(BF16), w_gate=w_gate.astype(BF16),
        w_up=w_up.astype(BF16), w_down=w_down.astype(BF16), ple_norm=ple_norm.astype(F32),
        w_ple_gate=w_ple_gate.astype(BF16), w_ple_proj=w_ple_proj.astype(BF16), lam_init=lam_init)
    y_prompt = _trunk(x_prompt, p_prompt, weights)
    y_sample = _trunk(x_sample, p_sample, weights)
    return (y_prompt, y_sample)
```

```python
import functools
import math

import jax
import jax.numpy as jnp
import numpy as np
from jax import lax
from jax.experimental import pallas as pl
from jax.experimental.pallas import tpu as pltpu

F32 = jnp.float32
BF16 = jnp.bfloat16
I32 = jnp.int32

D_MODEL = 1024
HEAD_DIM = 64
GRID_W = 64
NA_HEADS = 4
NA_ROWS = 8
NA_COLS = 16
SW_HEADS = 4
SW_KV_HEADS = 2
SW_WINDOW = 128
DIL_CONFIGS = ((128, 1), (512, 4), (2048, 16))
DIL_HEADS = 4
DIFF_HEADS = 4
DIFF_QK_DIM = HEAD_DIM // 2
ROPE_THETA = 10000.0
N_EXPERTS = 16
EC_CAPACITY_FACTOR = 2
PLE_DIM = 256
NEG_INF = -1e30
NORM_EPS = 1e-6

V7X_LANES = 128
V7X_VMEM_LIMIT = 56 * 1024 * 1024

CHUNK = 256
MAIN_COLS = 15 * CHUNK
N_IN_CHUNKS = 18
NA_QROWS = 2
NA_KROWS = NA_QROWS + NA_ROWS - 1
BAND_TQ = 128
DIFF_TQ = 512
DIFF_TK = 512
DIFF_FAST_BOUND = 40.0
LOG2_E = math.log2(math.e)
SLOT_BLK = 256
GATHER_TOK_BLK = 1024
COMBINE_TOK_BLK = 512
COMBINE_GROUP = 4


def _cparams(n_axes):
    return pltpu.CompilerParams(
        dimension_semantics=("arbitrary",) * n_axes, vmem_limit_bytes=V7X_VMEM_LIMIT)


_IN_CHUNK_KIND = (
    (64, 0), (64, 0), (0, 0),
    (64, 32), (64, 32),
    (64, 32), (64, 32), (64, 32),
    (64, 32), (64, 32), (64, 32),
    (0, 0), (0, 0), (0, 0),
    (32, 16),
    (32, 16),
    (0, 0), (0, 0),
)
_KBVB_CHUNK = 4
_KD_CHUNK = 15


def _in_proj_kernel(h_ref, gn_ref, w_ref, b64_ref, b32_ref, gains_ref, cs64_ref, sn64_ref,
                    cs32_ref, sn32_ref, main_ref, kdt_ref, vdx_ref):
    x = h_ref[...]
    ms = jnp.mean(x * x, axis=-1, keepdims=True)
    xn = (x * lax.rsqrt(ms + NORM_EPS) * gn_ref[...]).astype(BF16)
    tm = x.shape[0]
    lane = lax.broadcasted_iota(I32, (tm, CHUNK), 1)
    for c in range(N_IN_CHUNKS):
        hd, half = _IN_CHUNK_KIND[c]
        y = jnp.dot(xn, w_ref[:, c * CHUNK:(c + 1) * CHUNK], preferred_element_type=F32)
        out = y
        if hd:
            bmat = b64_ref[...] if hd == 64 else b32_ref[...]
            msq = jnp.dot((y * y).astype(BF16), bmat, preferred_element_type=F32)
            out = y * lax.rsqrt(msq + NORM_EPS) * gains_ref[c:c + 1, :]
        if half:
            cs = cs64_ref[...] if half == 32 else cs32_ref[...]
            sn = sn64_ref[...] if half == 32 else sn32_ref[...]
            fwd = pltpu.roll(out, half, 1)
            bwd = pltpu.roll(out, CHUNK - half, 1)
            first = (lane & (2 * half - 1)) < half
            out = out * cs + jnp.where(first, bwd, fwd) * sn
        if c == _KBVB_CHUNK:
            out = jnp.where(lane < CHUNK // 2, out, y)
        if c < 15:
            main_ref[:, c * CHUNK:(c + 1) * CHUNK] = out.astype(BF16)
        elif c == _KD_CHUNK:
            kdt_ref[0] = out.T.astype(BF16)
        else:
            ones = jnp.where((lane & (V7X_LANES - 1)) >= HEAD_DIM, 1.0, 0.0)
            vdx_ref[:, (c - 16) * CHUNK:(c - 15) * CHUNK] = (out + ones).astype(BF16)


def _in_proj(h2, gn, w_ext, b64, b32, gains, tabs, n_seq, seq_len, tm):
    n = h2.shape[0]
    nblk = seq_len // tm
    cs64, sn64, cs32, sn32 = tabs
    const = lambda i: (0, 0)
    tab = lambda i: (i % nblk, 0)
    return pl.pallas_call(
        _in_proj_kernel,
        grid=(n // tm,),
        in_specs=[
            pl.BlockSpec((tm, D_MODEL), lambda i: (i, 0)),
            pl.BlockSpec((1, D_MODEL), const),
            pl.BlockSpec((D_MODEL, N_IN_CHUNKS * CHUNK), const),
            pl.BlockSpec((CHUNK, CHUNK), const),
            pl.BlockSpec((CHUNK, CHUNK), const),
            pl.BlockSpec((N_IN_CHUNKS, CHUNK), const),
            pl.BlockSpec((tm, CHUNK), tab),
            pl.BlockSpec((tm, CHUNK), tab),
            pl.BlockSpec((tm, CHUNK), tab),
            pl.BlockSpec((tm, CHUNK), tab),
        ],
        out_specs=[
            pl.BlockSpec((tm, MAIN_COLS), lambda i: (i, 0)),
            pl.BlockSpec((1, CHUNK, tm), lambda i: (i // nblk, 0, i % nblk)),
            pl.BlockSpec((tm, 2 * CHUNK), lambda i: (i, 0)),
        ],
        out_shape=[
            jax.ShapeDtypeStruct((n, MAIN_COLS), BF16),
            jax.ShapeDtypeStruct((n_seq, CHUNK, seq_len), BF16),
            jax.ShapeDtypeStruct((n, 2 * CHUNK), BF16),
        ],
        compiler_params=_cparams(1),
        name="in_proj",
    )(h2, gn, w_ext, b64, b32, gains, cs64, sn64, cs32, sn32)


def _rope_tables(seq_len, half, reps):
    inv_freq = ROPE_THETA ** (-jnp.arange(half, dtype=F32) / half)
    ang = jnp.arange(seq_len, dtype=F32)[:, None] * inv_freq[None, :]
    cos, sin = jnp.cos(ang), jnp.sin(ang)
    cs = jnp.tile(jnp.concatenate([cos, cos], axis=-1), (1, reps))
    sn = jnp.tile(jnp.concatenate([-sin, sin], axis=-1), (1, reps))
    return cs, sn


def _block_mean_matrix(head):
    blocks = CHUNK // head
    return jnp.asarray(np.kron(np.eye(blocks), np.full((head, head), 1.0 / head)), BF16)


def _extend_w_in(w_in):
    vd = w_in[:, 4096:4352].reshape(D_MODEL, DIFF_HEADS, HEAD_DIM)
    vdx = jnp.concatenate([vd, jnp.zeros_like(vd)], axis=-1).reshape(D_MODEL, 2 * CHUNK)
    return jnp.concatenate([w_in[:, :4096], vdx], axis=-1).astype(BF16)


def _in_gains(qk_a, qk_b, qk_c, qk_d):
    one = jnp.ones((CHUNK,), F32)
    s64 = HEAD_DIM ** -0.5
    s32 = DIFF_QK_DIM ** -0.5
    t4 = lambda g: jnp.tile(g.astype(F32), 4)
    rows = [
        t4(qk_a[0]) * s64, t4(qk_a[1]), one,
        t4(qk_b[0]) * s64, jnp.concatenate([jnp.tile(qk_b[1].astype(F32), 2), jnp.ones((128,), F32)]),
        t4(qk_c[0]) * s64, t4(qk_c[0]) * s64, t4(qk_c[0]) * s64,
        t4(qk_c[1]), t4(qk_c[1]), t4(qk_c[1]),
        one, one, one,
        jnp.tile(qk_d[0].astype(F32), 8) * (s32 * LOG2_E),
        jnp.tile(qk_d[1].astype(F32), 8),
        one, one,
    ]
    return jnp.stack(rows)


def _na_kernel(kr0_ref, case_ref, q_ref, k_ref, v_ref, bias_ref, o_ref):
    del case_ref
    i = pl.program_id(1)
    ks = pl.multiple_of(kr0_ref[i] * GRID_W, GRID_W)
    nk = NA_KROWS * GRID_W
    q = q_ref[0]
    k = k_ref[0, pl.ds(ks, nk), :]
    v = v_ref[0, pl.ds(ks, nk), :]
    outs = []
    for h in range(NA_HEADS):
        sl = slice(h * HEAD_DIM, (h + 1) * HEAD_DIM)
        s = lax.dot_general(q[:, sl], k[:, sl], (((1,), (1,)), ((), ())),
                            preferred_element_type=F32)
        s = s + bias_ref[0, h]
        m = jnp.max(s, axis=-1, keepdims=True)
        e = jnp.exp(s - m)
        den = jnp.sum(e, axis=-1, keepdims=True)
        o = jnp.dot(e.astype(BF16), v[:, sl], preferred_element_type=F32)
        outs.append(o / den)
    o_ref[0] = jnp.concatenate(outs, axis=-1).astype(BF16)


def _na_plan(rows):
    kr, kc = NA_ROWS, NA_COLS
    nblk = rows // NA_QROWS
    kr0s, case_ids, cases = [], [], {}
    tables = []
    qi = np.arange(NA_QROWS * GRID_W)
    ki = np.arange(NA_KROWS * GRID_W)
    qr_l, qc = qi // GRID_W, qi % GRID_W
    kr_l, kcol = ki // GRID_W, ki % GRID_W
    cstart = np.clip(qc - kc // 2, 0, GRID_W - kc)
    for b in range(nblk):
        r0 = b * NA_QROWS
        kr0 = int(np.clip(r0 - kr // 2, 0, rows - NA_KROWS))
        rstart = np.clip(r0 + np.arange(NA_QROWS) - kr // 2, 0, rows - kr)
        key = (r0 - kr0, tuple(int(t) for t in rstart - kr0))
        if key not in cases:
            cases[key] = len(cases)
            q_row = r0 + qr_l
            k_row = kr0 + kr_l
            rs = rstart[qr_l]
            in_row = (k_row[None, :] >= rs[:, None]) & (k_row[None, :] < rs[:, None] + kr)
            in_col = (kcol[None, :] >= cstart[:, None]) & (kcol[None, :] < cstart[:, None] + kc)
            ir = np.clip(k_row[None, :] - q_row[:, None] + NA_ROWS - 1, 0, 2 * NA_ROWS - 2)
            ic = np.clip(kcol[None, :] - qc[:, None] + NA_COLS - 1, 0, 2 * NA_COLS - 2)
            tables.append((in_row & in_col, ir, ic))
        kr0s.append(kr0)
        case_ids.append(cases[key])
    mask = np.stack([t[0] for t in tables])
    ir = np.stack([t[1] for t in tables])
    ic = np.stack([t[2] for t in tables])
    return np.asarray(kr0s, np.int32), np.asarray(case_ids, np.int32), mask, ir, ic


def _na_bias(rpb, mask, ir, ic):
    n_layers, n_heads = rpb.shape[:2]
    n_case = ir.shape[0]
    ir_s = ir[:, ::GRID_W, ::GRID_W].reshape(-1)
    ic_s = ic[0, :GRID_W, :GRID_W].reshape(-1)
    oh_r = jnp.asarray(np.eye(2 * NA_ROWS - 1, dtype=np.float32)[ir_s])
    oh_c = jnp.asarray(np.eye(2 * NA_COLS - 1, dtype=np.float32)[ic_s].T)
    b = jnp.einsum("xa,lhab,bp->lhxp", oh_r, rpb.astype(F32), oh_c,
                   precision=lax.Precision.HIGHEST)
    b = b.reshape(n_layers, n_heads, n_case, NA_QROWS, NA_KROWS, GRID_W, GRID_W)
    b = b.transpose(0, 2, 1, 3, 5, 4, 6).reshape(
        n_layers, n_case, n_heads, NA_QROWS * GRID_W, NA_KROWS * GRID_W)
    return jnp.where(mask[None, :, None], b, NEG_INF)


def _na_attention(main3, bias, kr0s, case_ids):
    n_seq, seq_len, _ = main3.shape
    tq = NA_QROWS * GRID_W
    nk = NA_KROWS * GRID_W
    grid = (n_seq, seq_len // tq)
    return pl.pallas_call(
        _na_kernel,
        grid_spec=pltpu.PrefetchScalarGridSpec(
            num_scalar_prefetch=2,
            grid=grid,
            in_specs=[
                pl.BlockSpec((1, tq, CHUNK), lambda s, i, kr, cs: (s, i, 0)),
                pl.BlockSpec((1, seq_len, CHUNK), lambda s, i, kr, cs: (s, 0, 1)),
                pl.BlockSpec((1, seq_len, CHUNK), lambda s, i, kr, cs: (s, 0, 2)),
                pl.BlockSpec((1, NA_HEADS, tq, nk), lambda s, i, kr, cs: (cs[i], 0, 0, 0)),
            ],
            out_specs=pl.BlockSpec((1, tq, CHUNK), lambda s, i, kr, cs: (s, i, 0)),
        ),
        out_shape=jax.ShapeDtypeStruct((n_seq, seq_len, CHUNK), BF16),
        compiler_params=_cparams(2),
        name="mixer_a",
    )(jnp.asarray(kr0s), jnp.asarray(case_ids), main3, main3, main3, bias)


def _band_kernel(*refs, window, n_kv, rep, length, has_sink, want_lse):
    refs = list(refs)
    sink_ref = refs.pop(0) if has_sink else None
    q_ref, k_ref, v_ref = refs[:3]
    o_ref = refs[3]
    lse_ref = refs[4] if want_lse else None
    tq = q_ref.shape[1]
    kw = tq + 2 * window
    i = pl.program_id(1)
    q0 = i * tq
    ks = pl.multiple_of(jnp.clip(q0 - window, 0, length - kw), 64)
    q = q_ref[0]
    k = k_ref[0, pl.ds(ks, kw), :]
    v = v_ref[0, pl.ds(ks, kw), :]
    qpos = q0 + lax.broadcasted_iota(I32, (tq, kw), 0)
    kpos = ks + lax.broadcasted_iota(I32, (tq, kw), 1)
    valid = jnp.abs(qpos - kpos) <= window
    outs, lses = [], []
    for h in range(n_kv * rep):
        g = h // rep
        qs = slice(h * HEAD_DIM, (h + 1) * HEAD_DIM)
        kvs = slice(g * HEAD_DIM, (g + 1) * HEAD_DIM)
        s = lax.dot_general(q[:, qs], k[:, kvs], (((1,), (1,)), ((), ())),
                            preferred_element_type=F32)
        s = jnp.where(valid, s, NEG_INF)
        m = jnp.max(s, axis=-1, keepdims=True)
        if has_sink:
            m = jnp.maximum(m, sink_ref[h])
        e = jnp.exp(s - m)
        den = jnp.sum(e, axis=-1, keepdims=True)
        if has_sink:
            den = den + jnp.exp(sink_ref[h] - m)
        o = jnp.dot(e.astype(BF16), v[:, kvs], preferred_element_type=F32)
        outs.append(o / den)
        if want_lse:
            lses.append(jnp.broadcast_to(m + jnp.log(den), (tq, HEAD_DIM)))
    o_ref[0] = jnp.concatenate(outs, axis=-1).astype(BF16)
    if want_lse:
        lse_ref[0] = jnp.concatenate(lses, axis=-1)


def _band_attention(src3, q_blk, k_blk, v_blk, kv_width, dil, n_cols_blk, window, n_kv, rep,
                    sink=None, want_lse=False):
    n_seq, length, _ = src3.shape
    tq = BAND_TQ
    qw = n_kv * rep * HEAD_DIM
    n_q_blk, n_kv_blk = n_cols_blk
    kern = functools.partial(_band_kernel, window=window, n_kv=n_kv, rep=rep, length=length,
                             has_sink=sink is not None, want_lse=want_lse)
    in_specs = [
        pl.BlockSpec((1, tq, qw), lambda s, i: (s // dil, i, (s % dil) * n_q_blk + q_blk)),
        pl.BlockSpec((1, length, kv_width), lambda s, i: (s // dil, 0, (s % dil) * n_kv_blk + k_blk)),
        pl.BlockSpec((1, length, kv_width), lambda s, i: (s // dil, 0, (s % dil) * n_kv_blk + v_blk)),
    ]
    args = [src3, src3, src3]
    if sink is not None:
        in_specs = [pl.BlockSpec(memory_space=pltpu.SMEM)] + in_specs
        args = [sink.astype(F32)] + args
    o_spec = pl.BlockSpec((1, tq, qw), lambda s, i: (s // dil, i, s % dil))
    out_specs = [o_spec]
    out_shape = [jax.ShapeDtypeStruct((n_seq, length, dil * qw), BF16)]
    if want_lse:
        out_specs.append(o_spec)
        out_shape.append(jax.ShapeDtypeStruct((n_seq, length, dil * qw), F32))
    res = pl.pallas_call(
        kern,
        grid=(n_seq * dil, length // tq),
        in_specs=in_specs,
        out_specs=out_specs,
        out_shape=out_shape,
        compiler_params=_cparams(2),
        name="mixer_band",
    )(*args)
    return res


def _diff_kernel(lam_ref, q_ref, kt_ref, vx_ref, g_ref, o_ref, q_scr, m_scr, acc_scr, *, seq_len):
    tk = min(DIFF_TK, seq_len)
    n_sub = 2 * DIFF_HEADS
    q = q_ref[0]
    for j in range(n_sub):
        q_scr[j] = q[:, j * DIFF_QK_DIM:(j + 1) * DIFF_QK_DIM]
    acc_scr[...] = jnp.zeros(acc_scr.shape, F32)
    lam = lam_ref[0]
    post = lam_ref[1]
    bound = lam_ref[2]

    def operands(c, j):
        c0 = pl.multiple_of(c * tk, tk)
        kt = kt_ref[0, j * DIFF_QK_DIM:(j + 1) * DIFF_QK_DIM, pl.ds(c0, tk)]
        vv = vx_ref[0, pl.ds(c0, tk), (j // 2) * V7X_LANES:(j // 2 + 1) * V7X_LANES]
        return kt, vv

    @pl.when(bound <= DIFF_FAST_BOUND)
    def _():
        def body(c, carry):
            for j in range(n_sub):
                kt, vv = operands(c, j)
                s = jnp.dot(q_scr[j], kt, preferred_element_type=F32)
                p = jnp.exp2(s - bound).astype(BF16)
                acc_scr[j] += jnp.dot(p, vv, preferred_element_type=F32)
            return carry

        lax.fori_loop(0, seq_len // tk, body, 0)

    @pl.when(bound > DIFF_FAST_BOUND)
    def _():
        m_scr[...] = jnp.full(m_scr.shape, NEG_INF, F32)

        def body(c, carry):
            for j in range(n_sub):
                kt, vv = operands(c, j)
                s = jnp.dot(q_scr[j], kt, preferred_element_type=F32)
                mi = m_scr[j]
                mn = jnp.maximum(mi, jnp.max(s, axis=-1, keepdims=True))
                p = jnp.exp2(s - mn)
                alpha = jnp.exp2(mi - mn)
                acc_scr[j] = alpha * acc_scr[j] + jnp.dot(p.astype(BF16), vv,
                                                          preferred_element_type=F32)
                m_scr[j] = mn
            return carry

        lax.fori_loop(0, seq_len // tk, body, 0)

    outs = []
    for h in range(DIFF_HEADS):
        a0 = acc_scr[2 * h]
        a1 = acc_scr[2 * h + 1]
        od = a0[:, :HEAD_DIM] / a0[:, HEAD_DIM:] - lam * (a1[:, :HEAD_DIM] / a1[:, HEAD_DIM:])
        ms = jnp.mean(od * od, axis=-1, keepdims=True)
        outs.append(od * lax.rsqrt(ms + NORM_EPS) * g_ref[...] * post)
    o_ref[0] = jnp.concatenate(outs, axis=-1).astype(BF16)


def _diff_attention(main3, kdt, vdx3, lam_post, subln):
    n_seq, seq_len, _ = main3.shape
    tq = DIFF_TQ
    kern = functools.partial(_diff_kernel, seq_len=seq_len)
    return pl.pallas_call(
        kern,
        grid=(n_seq, seq_len // tq),
        in_specs=[
            pl.BlockSpec(memory_space=pltpu.SMEM),
            pl.BlockSpec((1, tq, CHUNK), lambda s, i: (s, i, 14)),
            pl.BlockSpec((1, CHUNK, seq_len), lambda s, i: (s, 0, 0),
                         pipeline_mode=pl.Buffered(1)),
            pl.BlockSpec((1, seq_len, 2 * CHUNK), lambda s, i: (s, 0, 0),
                         pipeline_mode=pl.Buffered(1)),
            pl.BlockSpec((1, HEAD_DIM), lambda s, i: (0, 0)),
        ],
        out_specs=pl.BlockSpec((1, tq, CHUNK), lambda s, i: (s, i, 0)),
        out_shape=jax.ShapeDtypeStruct((n_seq, seq_len, CHUNK), BF16),
        scratch_shapes=[
            pltpu.VMEM((2 * DIFF_HEADS, tq, DIFF_QK_DIM), BF16),
            pltpu.VMEM((2 * DIFF_HEADS, tq, 1), F32),
            pltpu.VMEM((2 * DIFF_HEADS, tq, V7X_LANES), F32),
        ],
        compiler_params=_cparams(2),
        name="mixer_d",
    )(lam_post, main3, kdt, vdx3, subln)


def _out_proj_kernel(h_ref, oa_ref, ob_ref, oc0_ref, oc1_ref, oc2_ref, l0_ref, l1_ref, l2_ref,
                     od_ref, w_ref, fn_ref, wr_ref, h1_ref, hn_ref, aff_ref):
    l0, l1, l2 = l0_ref[...], l1_ref[...], l2_ref[...]
    mx = jnp.maximum(jnp.maximum(l0, l1), l2)
    e0, e1, e2 = jnp.exp(l0 - mx), jnp.exp(l1 - mx), jnp.exp(l2 - mx)
    den = e0 + e1 + e2
    oc = ((e0 / den) * oc0_ref[...].astype(F32) + (e1 / den) * oc1_ref[...].astype(F32)
          + (e2 / den) * oc2_ref[...].astype(F32))
    acc = h_ref[...]
    acc = acc + jnp.dot(oa_ref[...], w_ref[0 * CHUNK:1 * CHUNK, :], preferred_element_type=F32)
    acc = acc + jnp.dot(ob_ref[...], w_ref[1 * CHUNK:2 * CHUNK, :], preferred_element_type=F32)
    acc = acc + jnp.dot(oc.astype(BF16), w_ref[2 * CHUNK:3 * CHUNK, :], preferred_element_type=F32)
    acc = acc + jnp.dot(od_ref[...], w_ref[3 * CHUNK:4 * CHUNK, :], preferred_element_type=F32)
    h1_ref[...] = acc
    ms = jnp.mean(acc * acc, axis=-1, keepdims=True)
    hn = (acc * lax.rsqrt(ms + NORM_EPS) * fn_ref[...]).astype(BF16)
    hn_ref[...] = hn
    logits = jnp.dot(hn, wr_ref[...], preferred_element_type=F32)
    m = jnp.max(logits, axis=-1, keepdims=True)
    e = jnp.exp(logits - m)
    aff_ref[...] = e / jnp.sum(e, axis=-1, keepdims=True)


def _out_proj(h2, oa, ob, ocs, lses, od, w_out, ffn_norm, w_router, tm):
    n = h2.shape[0]
    row = lambda w: pl.BlockSpec((tm, w), lambda i: (i, 0))
    const = lambda r, c: pl.BlockSpec((r, c), lambda i: (0, 0))
    return pl.pallas_call(
        _out_proj_kernel,
        grid=(n // tm,),
        in_specs=[row(D_MODEL)] + [row(CHUNK)] * 9 + [
            const(D_MODEL, D_MODEL), const(1, D_MODEL), const(D_MODEL, N_EXPERTS)],
        out_specs=[row(D_MODEL), row(D_MODEL), row(N_EXPERTS)],
        out_shape=[
            jax.ShapeDtypeStruct((n, D_MODEL), F32),
            jax.ShapeDtypeStruct((n, D_MODEL), BF16),
            jax.ShapeDtypeStruct((n, N_EXPERTS), F32),
        ],
        compiler_params=_cparams(1),
        name="out_proj",
    )(h2, oa, ob, ocs[0], ocs[1], ocs[2], lses[0], lses[1], lses[2], od, w_out, ffn_norm,
      w_router)


def _select_kernel(aff_ref, posm_ref, pref_ref, *, cap, n_tok):
    def bits():
        return lax.bitcast_convert_type(aff_ref[...], I32)

    def bit_body(i, thr):
        cand = thr | jnp.left_shift(jnp.int32(1), 30 - i)
        cnt = jnp.sum(jnp.where(bits() >= cand, 1.0, 0.0), axis=-1, keepdims=True)
        return jnp.where(cnt >= cap, cand, thr)

    thr = lax.fori_loop(0, 31, bit_body, jnp.zeros((N_EXPERTS, 1), I32))
    cnt_gt = jnp.sum(jnp.where(bits() > thr, 1.0, 0.0), axis=-1, keepdims=True)
    need = cap - cnt_gt
    blk = SLOT_BLK
    r = lax.broadcasted_iota(I32, (blk, blk), 0)
    c = lax.broadcasted_iota(I32, (blk, blk), 1)
    upper = jnp.where(r < c, 1.0, 0.0).astype(BF16)

    def blk_body(b, carry):
        ceq, csel = carry
        b0 = pl.multiple_of(b * blk, blk)
        xb = lax.bitcast_convert_type(aff_ref[:, pl.ds(b0, blk)], I32)
        gt = xb > thr
        eq = xb == thr
        eqf = jnp.where(eq, 1.0, 0.0)
        rank = ceq + jnp.dot(eqf.astype(BF16), upper, preferred_element_type=F32)
        sel = gt | (eq & (rank < need))
        self_ = jnp.where(sel, 1.0, 0.0)
        pref = csel + jnp.dot(self_.astype(BF16), upper, preferred_element_type=F32)
        prefi = pref.astype(I32)
        pref_ref[:, pl.ds(b0, blk)] = prefi
        posm_ref[:, pl.ds(b0, blk)] = jnp.where(sel, prefi, -1)
        return (ceq + jnp.sum(eqf, axis=-1, keepdims=True),
                csel + jnp.sum(self_, axis=-1, keepdims=True))

    zero = jnp.zeros((N_EXPERTS, 1), F32)
    lax.fori_loop(0, n_tok // blk, blk_body, (zero, zero))


def _select(aff_t, cap):
    n_tok = aff_t.shape[1]
    kern = functools.partial(_select_kernel, cap=cap, n_tok=n_tok)
    full = pl.BlockSpec((N_EXPERTS, n_tok), lambda i: (0, 0))
    return pl.pallas_call(
        kern,
        grid=(1,),
        in_specs=[full],
        out_specs=[full, full],
        out_shape=[jax.ShapeDtypeStruct((N_EXPERTS, n_tok), I32)] * 2,
        compiler_params=_cparams(1),
        name="select",
    )(aff_t)


_VALID, _FIRST, _LAST = 1, 2, 4


_NO_HIT = 1 << 30


def _block_windows(pref, cap, tok_blk):
    n_exp = pref.shape[0]
    nw = cap // SLOT_BLK
    off = pref[:, ::tok_blk]
    end = jnp.concatenate([off[:, 1:], jnp.full((n_exp, 1), cap, I32)], axis=1)
    cnt = end - off
    w_lo = jnp.minimum(off // SLOT_BLK, nw - 1)
    w_hi = jnp.where(cnt > 0, (end - 1) // SLOT_BLK, w_lo)
    return cnt, w_lo, w_hi


def _search(starts, idx):
    return (jnp.searchsorted(starts, idx, side="right", method="compare_all") - 1).astype(I32)


def _gather_plan(pref, cap):
    n_exp, n_tok = pref.shape
    nb = n_tok // GATHER_TOK_BLK
    nw = cap // SLOT_BLK
    _, w_lo, w_hi = _block_windows(pref, cap, GATHER_TOK_BLK)
    s_g = nb + nw
    nst = w_hi - w_lo + 1
    start = jnp.cumsum(nst, axis=1) - nst
    total = jnp.sum(nst, axis=1, keepdims=True)
    s_idx = jnp.arange(s_g, dtype=I32)[None, :]
    s_eff = jnp.minimum(s_idx, total - 1)
    b_s = jax.vmap(_search)(start, s_eff)
    w_s = jnp.take_along_axis(w_lo, b_s, axis=1) + (s_eff - jnp.take_along_axis(start, b_s, axis=1))
    valid = s_idx < total
    prev_w = jnp.concatenate([jnp.full((n_exp, 1), -1, I32), w_s[:, :-1]], axis=1)
    next_w = jnp.concatenate([w_s[:, 1:], jnp.full((n_exp, 1), -1, I32)], axis=1)
    first = valid & (w_s != prev_w)
    last = valid & ((w_s != next_w) | (s_idx == total - 1))
    flag = valid * _VALID + first * _FIRST + last * _LAST
    return (b_s.reshape(-1), w_s.reshape(-1).astype(I32), flag.reshape(-1).astype(I32)), s_g


def _combine_plan(pref, cap):
    n_exp, n_tok = pref.shape
    nb = n_tok // COMBINE_TOK_BLK
    nw = cap // SLOT_BLK
    grp = COMBINE_GROUP
    cnt, w_lo, w_hi = _block_windows(pref, cap, COMBINE_TOK_BLK)
    cnt_t, wlo_t, whi_t = cnt.T, w_lo.T, w_hi.T
    n_win = jnp.where(cnt_t > 0, whi_t - wlo_t + 1, 0)
    win_start = (jnp.cumsum(n_win.reshape(-1)) - n_win.reshape(-1)).astype(I32)
    per_blk = jnp.sum(n_win, axis=1)
    blk_first_win = (jnp.cumsum(per_blk) - per_blk).astype(I32)
    steps = jnp.maximum((per_blk + grp - 1) // grp, 1)
    step_start = (jnp.cumsum(steps) - steps).astype(I32)
    total = jnp.sum(steps)
    s_c = (n_exp * (nb + nw) + grp - 1) // grp + nb
    s_idx = jnp.arange(s_c, dtype=I32)
    s_eff = jnp.minimum(s_idx, total - 1)
    cb = _search(step_start, s_eff)
    local = s_eff - step_start[cb]
    q = local[:, None] * grp + jnp.arange(grp, dtype=I32)[None, :]
    live = (q < per_blk[cb][:, None]) & (s_idx < total)[:, None]
    win = jnp.where(live, blk_first_win[cb][:, None] + q, 0)
    pair = _search(win_start, win.reshape(-1)).reshape(win.shape)
    ce = pair % n_exp
    cw = wlo_t.reshape(-1)[pair] + (win - win_start[pair])
    cw = jnp.clip(cw, 0, nw - 1)
    blk = (ce * nw + cw).astype(I32)
    base = jnp.where(live, cw * SLOT_BLK, _NO_HIT).astype(I32)
    valid = s_idx < total
    prev_b = jnp.concatenate([jnp.full((1,), -1, I32), cb[:-1]])
    next_b = jnp.concatenate([cb[1:], jnp.full((1,), -1, I32)])
    first = valid & (cb != prev_b)
    last = valid & ((cb != next_b) | (s_idx == total - 1))
    flag = (valid * _VALID + first * _FIRST + last * _LAST).astype(I32)
    return (cb, ce.reshape(-1).astype(I32), blk.reshape(-1), base.reshape(-1), flag), s_c


def _gather_kernel(gb_ref, gw_ref, gf_ref, posm_ref, g_ref, hn_ref, xe_ref, gs_ref, acc_ref,
                   gacc_ref, *, n_steps):
    e = pl.program_id(0)
    s = pl.program_id(1)
    flag = gf_ref[e * n_steps + s]
    w = gw_ref[e * n_steps + s]

    @pl.when((flag & _FIRST) != 0)
    def _():
        acc_ref[...] = jnp.zeros_like(acc_ref)
        gacc_ref[...] = jnp.zeros_like(gacc_ref)

    @pl.when((flag & _VALID) != 0)
    def _():
        rel = posm_ref[0] - w * SLOT_BLK
        hit = lax.broadcasted_iota(I32, (SLOT_BLK, GATHER_TOK_BLK), 0) == rel
        onehot = jnp.where(hit, 1.0, 0.0).astype(BF16)
        acc_ref[...] += jnp.dot(onehot, hn_ref[...], preferred_element_type=F32)
        gacc_ref[...] += jnp.sum(jnp.where(hit, g_ref[0], 0.0), axis=-1, keepdims=True)

    @pl.when((flag & _LAST) != 0)
    def _():
        xe_ref[0] = acc_ref[...].astype(BF16)
        gs_ref[0] = gacc_ref[...]


def _gather(plan, s_g, posm3, g3, hn, cap):
    gb, gw, gf = plan
    n_tok = hn.shape[0]
    kern = functools.partial(_gather_kernel, n_steps=s_g)
    tok = lambda e, s, gb, gw, gf: (e, 0, gb[e * s_g + s])
    slot = lambda e, s, gb, gw, gf: (e, gw[e * s_g + s], 0)
    return pl.pallas_call(
        kern,
        grid_spec=pltpu.PrefetchScalarGridSpec(
            num_scalar_prefetch=3,
            grid=(N_EXPERTS, s_g),
            in_specs=[
                pl.BlockSpec((1, 1, GATHER_TOK_BLK), tok),
                pl.BlockSpec((1, 1, GATHER_TOK_BLK), tok),
                pl.BlockSpec((GATHER_TOK_BLK, D_MODEL), lambda e, s, gb, gw, gf: (gb[e * s_g + s], 0)),
            ],
            out_specs=[
                pl.BlockSpec((1, SLOT_BLK, D_MODEL), slot),
                pl.BlockSpec((1, SLOT_BLK, 1), slot),
            ],
            scratch_shapes=[pltpu.VMEM((SLOT_BLK, D_MODEL), F32), pltpu.VMEM((SLOT_BLK, 1), F32)],
        ),
        out_shape=[
            jax.ShapeDtypeStruct((N_EXPERTS, cap, D_MODEL), BF16),
            jax.ShapeDtypeStruct((N_EXPERTS, cap, 1), F32),
        ],
        compiler_params=_cparams(2),
        name="gather",
    )(gb, gw, gf, posm3, g3, hn)


def _experts_kernel(x_ref, gs_ref, wg_ref, wu_ref, wd_ref, y_ref):
    x = x_ref[0]
    hg = jnp.dot(x, wg_ref[0], preferred_element_type=F32)
    hu = jnp.dot(x, wu_ref[0], preferred_element_type=F32)
    hdn = (hg / (1.0 + jnp.exp(-hg)) * hu).astype(BF16)
    y = jnp.dot(hdn, wd_ref[0], preferred_element_type=F32)
    y_ref[0] = (y * gs_ref[0]).astype(BF16)


def _experts(xe, gs, w_gate, w_up, w_down):
    n_exp, cap, _ = xe.shape
    tc = min(512, cap)
    wspec = pl.BlockSpec((1, D_MODEL, D_MODEL), lambda e, c: (e, 0, 0))
    return pl.pallas_call(
        _experts_kernel,
        grid=(n_exp, cap // tc),
        in_specs=[
            pl.BlockSpec((1, tc, D_MODEL), lambda e, c: (e, c, 0)),
            pl.BlockSpec((1, tc, 1), lambda e, c: (e, c, 0)),
            wspec, wspec, wspec,
        ],
        out_specs=pl.BlockSpec((1, tc, D_MODEL), lambda e, c: (e, c, 0)),
        out_shape=jax.ShapeDtypeStruct((n_exp, cap, D_MODEL), BF16),
        compiler_params=_cparams(2),
        name="experts",
    )(xe, gs, w_gate, w_up, w_down)


def _combine_kernel(cb_ref, ce_ref, cblk_ref, cbase_ref, cf_ref, h1_ref, posmt_ref, *rest):
    del cb_ref, cblk_ref
    grp = COMBINE_GROUP
    ye_refs = rest[:grp]
    p_ref, pn_ref, wpg_ref, wpp_ref, o_ref, acc_ref = rest[grp:]
    s = pl.program_id(0)
    flag = cf_ref[s]

    @pl.when((flag & _FIRST) != 0)
    def _():
        acc_ref[...] = jnp.zeros_like(acc_ref)

    @pl.when((flag & _VALID) != 0)
    def _():
        pm = posmt_ref[...].astype(F32)
        lane = lax.broadcasted_iota(I32, pm.shape, 1)
        slot_iota = lax.broadcasted_iota(I32, (COMBINE_TOK_BLK, SLOT_BLK), 1)
        total = None
        for k in range(grp):
            e = ce_ref[s * grp + k]
            col = jnp.sum(jnp.where(lane == e, pm, 0.0), axis=-1, keepdims=True)
            rel = col.astype(I32) - cbase_ref[s * grp + k]
            onehot = jnp.where(slot_iota == rel, 1.0, 0.0).astype(BF16)
            part = jnp.dot(onehot, ye_refs[k][0], preferred_element_type=F32)
            total = part if total is None else total + part
        acc_ref[...] += total

    @pl.when((flag & _LAST) != 0)
    def _():
        h2 = h1_ref[...] + acc_ref[...]
        ms = jnp.mean(h2 * h2, axis=-1, keepdims=True)
        hn = (h2 * lax.rsqrt(ms + NORM_EPS) * pn_ref[...]).astype(BF16)
        z = jnp.dot(hn, wpg_ref[...], preferred_element_type=F32)
        gate = 1.0 / (1.0 + jnp.exp(-z))
        proj = jnp.dot(p_ref[...].astype(BF16), wpp_ref[...], preferred_element_type=F32)
        o_ref[...] = h2 + gate * proj


def _combine(plan, s_c, h1, posm_t, ye, p2, ple_norm, w_ple_gate, w_ple_proj, cap):
    cb, ce, cblk, cbase, cf = plan
    n_tok = h1.shape[0]
    nw = cap // SLOT_BLK
    grp = COMBINE_GROUP
    ye3 = ye.reshape(N_EXPERTS * nw, SLOT_BLK, D_MODEL)
    tokb = lambda width: pl.BlockSpec((COMBINE_TOK_BLK, width), lambda s, cb, *_: (cb[s], 0))
    const = lambda r, c: pl.BlockSpec((r, c), lambda s, *_: (0, 0))

    def ye_spec(k):
        return pl.BlockSpec((1, SLOT_BLK, D_MODEL),
                            lambda s, cb, ce, cblk, *_: (cblk[s * grp + k], 0, 0))

    return pl.pallas_call(
        _combine_kernel,
        grid_spec=pltpu.PrefetchScalarGridSpec(
            num_scalar_prefetch=5,
            grid=(s_c,),
            in_specs=[tokb(D_MODEL), tokb(N_EXPERTS)] + [ye_spec(k) for k in range(grp)] + [
                tokb(PLE_DIM), const(1, D_MODEL), const(D_MODEL, D_MODEL),
                const(PLE_DIM, D_MODEL)],
            out_specs=tokb(D_MODEL),
            scratch_shapes=[pltpu.VMEM((COMBINE_TOK_BLK, D_MODEL), F32)],
        ),
        out_shape=jax.ShapeDtypeStruct((n_tok, D_MODEL), F32),
        compiler_params=_cparams(1),
        name="combine",
    )(cb, ce, cblk, cbase, cf, h1, posm_t, *([ye3] * grp), p2, ple_norm, w_ple_gate, w_ple_proj)


def _layer(h2, p2, lw, static):
    n_seq, seq_len, tabs, b64, b32, na_plan = static
    n_tok = h2.shape[0]
    tm = min(512, seq_len)
    gains = _in_gains(lw["qk_norm_a"], lw["qk_norm_b"], lw["qk_norm_c"], lw["qk_norm_d"])
    main, kdt, vdx = _in_proj(h2, lw["attn_norm"][None, :], lw["w_in"], b64, b32, gains, tabs,
                              n_seq, seq_len, tm)
    main3 = main.reshape(n_seq, seq_len, MAIN_COLS)

    kr0s, case_ids = na_plan
    oa = _na_attention(main3, lw["na_bias"], kr0s, case_ids)

    (ob,) = _band_attention(main3, 3, 8, 9, V7X_LANES, 1, (MAIN_COLS // CHUNK, MAIN_COLS // V7X_LANES),
                            SW_WINDOW, SW_KV_HEADS, SW_HEADS // SW_KV_HEADS, sink=lw["sink_b"])
    ocs, lses = [], []
    for g_i, (win, dil) in enumerate(DIL_CONFIGS):
        src = main.reshape(n_seq, seq_len // dil, dil * MAIN_COLS)
        nblk = MAIN_COLS // CHUNK
        o_g, l_g = _band_attention(src, 5 + g_i, 8 + g_i, 11 + g_i, CHUNK, dil, (nblk, nblk),
                                   win // (2 * dil), DIL_HEADS, 1, want_lse=True)
        ocs.append(o_g.reshape(n_tok, CHUNK))
        lses.append(l_g.reshape(n_tok, CHUNK))

    lf = lw["lambda_d"].astype(F32)
    lam = jnp.exp(jnp.sum(lf[0] * lf[1])) - jnp.exp(jnp.sum(lf[2] * lf[3])) + lw["lam_init"]
    gd = jnp.abs(lw["qk_norm_d"].astype(F32))
    score_bound = (LOG2_E * DIFF_QK_DIM ** 0.5 * (1.0 + 2.0 ** -6)) * jnp.max(gd[0]) * jnp.max(gd[1])
    lam_post = jnp.stack([lam, 1.0 - lw["lam_init"], score_bound]).astype(F32)
    od = _diff_attention(main3, kdt, vdx.reshape(n_seq, seq_len, 2 * CHUNK), lam_post,
                         lw["subln_d"].astype(F32)[None, :])

    h1, hn, aff = _out_proj(h2, oa.reshape(n_tok, CHUNK), ob.reshape(n_tok, CHUNK), ocs, lses,
                            od.reshape(n_tok, CHUNK), lw["w_out"], lw["ffn_norm"][None, :],
                            lw["w_router"], tm)

    cap = EC_CAPACITY_FACTOR * n_tok // N_EXPERTS
    aff_t = aff.T
    posm, pref = _select(aff_t, cap)
    gather_plan, s_g = _gather_plan(pref, cap)
    combine_plan, s_c = _combine_plan(pref, cap)
    xe, gs = _gather(gather_plan, s_g, posm.reshape(N_EXPERTS, 1, n_tok),
                     aff_t.reshape(N_EXPERTS, 1, n_tok), hn, cap)
    ye = _experts(xe, gs, lw["w_gate"], lw["w_up"], lw["w_down"])
    return _combine(combine_plan, s_c, h1, posm.T, ye, p2, lw["ple_norm"][None, :],
                    lw["w_ple_gate"], lw["w_ple_proj"], cap)


def _trunk(x, p, weights):
    n_seq, seq_len, _ = x.shape
    depth = p.shape[0]
    tabs = _rope_tables(seq_len, HEAD_DIM // 2, 4) + _rope_tables(seq_len, DIFF_QK_DIM // 2, 8)
    kr0s, case_ids, mask, ir, ic = _na_plan(seq_len // GRID_W)
    static = (n_seq, seq_len, tabs, _block_mean_matrix(HEAD_DIM), _block_mean_matrix(DIFF_QK_DIM),
              (kr0s, case_ids))
    xs = dict(weights)
    xs["p"] = p.reshape(depth, n_seq * seq_len, PLE_DIM)
    xs["na_bias"] = _na_bias(weights["rpb_a"], mask, ir, ic)

    def step(h2, lw):
        return _layer(h2, lw["p"], lw, static), None

    h2, _ = lax.scan(step, x.reshape(n_seq * seq_len, D_MODEL), xs)
    return h2.reshape(n_seq, seq_len, D_MODEL)


def kernel(x_prompt, x_sample, p_prompt, p_sample, attn_norm, w_in, rpb_a, qk_norm_a, qk_norm_b, sink_b, qk_norm_c, qk_norm_d, lambda_d, subln_d, w_out, ffn_norm, w_router, w_gate, w_up, w_down, ple_norm, w_ple_gate, w_ple_proj):
    depth = w_in.shape[0]
    lam_init = jnp.asarray([0.8 - 0.6 * math.exp(-0.3 * i) for i in range(depth)], F32)
    weights = dict(
        attn_norm=attn_norm.astype(F32), w_in=jax.vmap(_extend_w_in)(w_in), rpb_a=rpb_a,
        qk_norm_a=qk_norm_a, qk_norm_b=qk_norm_b, sink_b=sink_b, qk_norm_c=qk_norm_c,
        qk_norm_d=qk_norm_d, lambda_d=lambda_d, subln_d=subln_d, w_out=w_out.astype(BF16),
        ffn_norm=ffn_norm.astype(F32), w_router=w_router.astype(BF16), w_gate=w_gate.astype(BF16),
        w_up=w_up.astype(BF16), w_down=w_down.astype(BF16), ple_norm=ple_norm.astype(F32),
        w_ple_gate=w_ple_gate.astype(BF16), w_ple_proj=w_ple_proj.astype(BF16), lam_init=lam_init)
    y_prompt = _trunk(x_prompt, p_prompt, weights)
    y_sample = _trunk(x_sample, p_sample, weights)
    return (y_prompt, y_sample)
```

```python
import functools
import math

import jax
import jax.numpy as jnp
import numpy as np
from jax import lax
from jax.experimental import pallas as pl
from jax.experimental.pallas import tpu as pltpu

F32 = jnp.float32
BF16 = jnp.bfloat16
I32 = jnp.int32

D_MODEL = 1024
HEAD_DIM = 64
GRID_W = 64
NA_HEADS = 4
NA_ROWS = 8
NA_COLS = 16
SW_HEADS = 4
SW_KV_HEADS = 2
SW_WINDOW = 128
DIL_CONFIGS = ((128, 1), (512, 4), (2048, 16))
DIL_HEADS = 4
DIFF_HEADS = 4
DIFF_QK_DIM = HEAD_DIM // 2
ROPE_THETA = 10000.0
N_EXPERTS = 16
EC_CAPACITY_FACTOR = 2
PLE_DIM = 256
NEG_INF = -1e30
NORM_EPS = 1e-6

V7X_LANES = 128
V7X_VMEM_LIMIT = 56 * 1024 * 1024

CHUNK = 256
MAIN_COLS = 9 * CHUNK
N_IN_CHUNKS = 18
NA_QROWS = 2
NA_KROWS = NA_QROWS + NA_ROWS - 1
NA_SUB_BLOCKS = 4
BAND_STEP_ROWS = 512
DIFF_TQ = 512
DIFF_TK = 512
DIFF_FAST_BOUND = 40.0
LOG2_E = math.log2(math.e)
SLOT_BLK = 256
GATHER_TOK_BLK = 1024
COMBINE_TOK_BLK = 512
COMBINE_GROUP = 4


def _cparams(n_axes):
    return pltpu.CompilerParams(
        dimension_semantics=("arbitrary",) * n_axes, vmem_limit_bytes=V7X_VMEM_LIMIT)


_IN_CHUNK_KIND = (
    (64, 0), (64, 0), (0, 0),
    (64, 32), (64, 32),
    (64, 32), (64, 32), (64, 32),
    (64, 32), (64, 32), (64, 32),
    (0, 0), (0, 0), (0, 0),
    (32, 16),
    (32, 16),
    (0, 0), (0, 0),
)
_KBVB_CHUNK = 4
_IN_CHUNK_DEST = (
    ("main", 0), ("main", 1), ("main", 2), ("main", 3), ("main", 4),
    ("main", 5), ("dil", (0, 0)), ("dil", (1, 0)),
    ("main", 6), ("dil", (0, 1)), ("dil", (1, 1)),
    ("main", 7), ("dil", (0, 2)), ("dil", (1, 2)),
    ("main", 8), ("kdt", 0), ("vdx", 0), ("vdx", 1),
)


def _in_proj_kernel(h_ref, gn_ref, w_ref, b64_ref, b32_ref, gains_ref, cs64_ref, sn64_ref,
                    cs32_ref, sn32_ref, main_ref, cg1_ref, cg2_ref, kdt_ref, vdx_ref, row_scr):
    x = h_ref[...]
    ms = jnp.mean(x * x, axis=-1, keepdims=True)
    xn = (x * lax.rsqrt(ms + NORM_EPS) * gn_ref[...]).astype(BF16)
    tm = x.shape[0]
    lane = lax.broadcasted_iota(I32, (tm, CHUNK), 1)
    for c in range(N_IN_CHUNKS):
        hd, half = _IN_CHUNK_KIND[c]
        y = jnp.dot(xn, w_ref[:, c * CHUNK:(c + 1) * CHUNK], preferred_element_type=F32)
        out = y
        if hd:
            bmat = b64_ref[...] if hd == 64 else b32_ref[...]
            msq = jnp.dot((y * y).astype(BF16), bmat, preferred_element_type=F32)
            out = y * lax.rsqrt(msq + NORM_EPS) * gains_ref[c:c + 1, :]
        if half:
            cs = cs64_ref[...] if half == 32 else cs32_ref[...]
            sn = sn64_ref[...] if half == 32 else sn32_ref[...]
            fwd = pltpu.roll(out, half, 1)
            bwd = pltpu.roll(out, CHUNK - half, 1)
            first = (lane & (2 * half - 1)) < half
            out = out * cs + jnp.where(first, bwd, fwd) * sn
        if c == _KBVB_CHUNK:
            out = jnp.where(lane < CHUNK // 2, out, y)
        kind, where = _IN_CHUNK_DEST[c]
        if kind == "main":
            main_ref[:, where * CHUNK:(where + 1) * CHUNK] = out.astype(BF16)
        elif kind == "dil":
            group, part = where
            dst = (cg1_ref, cg2_ref)[group]
            dil = dst.shape[1]
            for half_i in range(CHUNK // V7X_LANES):
                lanes = slice(half_i * V7X_LANES, (half_i + 1) * V7X_LANES)
                row_scr[half_i] = out[:, lanes]
                for r in range(dil):
                    col0 = part * CHUNK + half_i * V7X_LANES
                    dst[0, r, :, col0:col0 + V7X_LANES] = row_scr.at[half_i][
                        pl.ds(r, tm // dil, stride=dil), :].astype(BF16)
        elif kind == "kdt":
            kdt_ref[0] = out.T.astype(BF16)
        else:
            ones = jnp.where((lane & (V7X_LANES - 1)) >= HEAD_DIM, 1.0, 0.0)
            vdx_ref[:, where * CHUNK:(where + 1) * CHUNK] = (out + ones).astype(BF16)


def _dilated_spec(dil, tm, width, nblk):
    return pl.BlockSpec((1, dil, tm // dil, width), lambda i: (i // nblk, 0, i % nblk, 0))


def _in_proj(h2, gn, w_ext, b64, b32, gains, tabs, n_seq, seq_len, tm):
    n = h2.shape[0]
    nblk = seq_len // tm
    d1, d2 = DIL_CONFIGS[1][1], DIL_CONFIGS[2][1]
    cs64, sn64, cs32, sn32 = tabs
    const = lambda i: (0, 0)
    tab = lambda i: (i % nblk, 0)
    return pl.pallas_call(
        _in_proj_kernel,
        grid=(n // tm,),
        in_specs=[
            pl.BlockSpec((tm, D_MODEL), lambda i: (i, 0)),
            pl.BlockSpec((1, D_MODEL), const),
            pl.BlockSpec((D_MODEL, N_IN_CHUNKS * CHUNK), const),
            pl.BlockSpec((CHUNK, CHUNK), const),
            pl.BlockSpec((CHUNK, CHUNK), const),
            pl.BlockSpec((N_IN_CHUNKS, CHUNK), const),
            pl.BlockSpec((tm, CHUNK), tab),
            pl.BlockSpec((tm, CHUNK), tab),
            pl.BlockSpec((tm, CHUNK), tab),
            pl.BlockSpec((tm, CHUNK), tab),
        ],
        out_specs=[
            pl.BlockSpec((tm, MAIN_COLS), lambda i: (i, 0)),
            _dilated_spec(d1, tm, 3 * CHUNK, nblk),
            _dilated_spec(d2, tm, 3 * CHUNK, nblk),
            pl.BlockSpec((1, CHUNK, tm), lambda i: (i // nblk, 0, i % nblk)),
            pl.BlockSpec((tm, 2 * CHUNK), lambda i: (i, 0)),
        ],
        out_shape=[
            jax.ShapeDtypeStruct((n, MAIN_COLS), BF16),
            jax.ShapeDtypeStruct((n_seq, d1, seq_len // d1, 3 * CHUNK), BF16),
            jax.ShapeDtypeStruct((n_seq, d2, seq_len // d2, 3 * CHUNK), BF16),
            jax.ShapeDtypeStruct((n_seq, CHUNK, seq_len), BF16),
            jax.ShapeDtypeStruct((n, 2 * CHUNK), BF16),
        ],
        scratch_shapes=[pltpu.VMEM((CHUNK // V7X_LANES, tm, V7X_LANES), F32)],
        compiler_params=_cparams(1),
        name="in_proj",
    )(h2, gn, w_ext, b64, b32, gains, cs64, sn64, cs32, sn32)


def _rope_tables(seq_len, half, reps):
    inv_freq = ROPE_THETA ** (-jnp.arange(half, dtype=F32) / half)
    ang = jnp.arange(seq_len, dtype=F32)[:, None] * inv_freq[None, :]
    cos, sin = jnp.cos(ang), jnp.sin(ang)
    cs = jnp.tile(jnp.concatenate([cos, cos], axis=-1), (1, reps))
    sn = jnp.tile(jnp.concatenate([-sin, sin], axis=-1), (1, reps))
    return cs, sn


def _block_mean_matrix(head):
    blocks = CHUNK // head
    return jnp.asarray(np.kron(np.eye(blocks), np.full((head, head), 1.0 / head)), BF16)


def _extend_w_in(w_in):
    vd = w_in[:, 4096:4352].reshape(D_MODEL, DIFF_HEADS, HEAD_DIM)
    vdx = jnp.concatenate([vd, jnp.zeros_like(vd)], axis=-1).reshape(D_MODEL, 2 * CHUNK)
    return jnp.concatenate([w_in[:, :4096], vdx], axis=-1).astype(BF16)


def _in_gains(qk_a, qk_b, qk_c, qk_d):
    one = jnp.ones((CHUNK,), F32)
    s64 = HEAD_DIM ** -0.5
    s32 = DIFF_QK_DIM ** -0.5
    t4 = lambda g: jnp.tile(g.astype(F32), 4)
    rows = [
        t4(qk_a[0]) * s64, t4(qk_a[1]), one,
        t4(qk_b[0]) * s64, jnp.concatenate([jnp.tile(qk_b[1].astype(F32), 2), jnp.ones((128,), F32)]),
        t4(qk_c[0]) * s64, t4(qk_c[0]) * s64, t4(qk_c[0]) * s64,
        t4(qk_c[1]), t4(qk_c[1]), t4(qk_c[1]),
        one, one, one,
        jnp.tile(qk_d[0].astype(F32), 8) * (s32 * LOG2_E),
        jnp.tile(qk_d[1].astype(F32), 8),
        one, one,
    ]
    return jnp.stack(rows)


def _na_kernel(kr0_ref, case_ref, q_ref, k_ref, v_ref, bias_ref, o_ref):
    i = pl.program_id(1)
    tq = NA_QROWS * GRID_W
    nk = NA_KROWS * GRID_W
    n_sub = q_ref.shape[1] // tq
    for sb in range(n_sub):
        blk = i * n_sub + sb
        rows = slice(sb * tq, (sb + 1) * tq)
        ks = pl.multiple_of(kr0_ref[blk] * GRID_W, GRID_W)
        case = case_ref[blk]
        q = q_ref[0, rows, :]
        k = k_ref[0, pl.ds(ks, nk), :]
        v = v_ref[0, pl.ds(ks, nk), :]
        outs = []
        for h in range(NA_HEADS):
            sl = slice(h * HEAD_DIM, (h + 1) * HEAD_DIM)
            s = lax.dot_general(q[:, sl], k[:, sl], (((1,), (1,)), ((), ())),
                                preferred_element_type=F32)
            s = s + bias_ref[case, h]
            m = jnp.max(s, axis=-1, keepdims=True)
            e = jnp.exp(s - m)
            den = jnp.sum(e, axis=-1, keepdims=True)
            o = jnp.dot(e.astype(BF16), v[:, sl], preferred_element_type=F32)
            outs.append(o / den)
        o_ref[0, rows, :] = jnp.concatenate(outs, axis=-1).astype(BF16)


def _na_plan(rows):
    kr, kc = NA_ROWS, NA_COLS
    nblk = rows // NA_QROWS
    kr0s, case_ids, cases = [], [], {}
    tables = []
    qi = np.arange(NA_QROWS * GRID_W)
    ki = np.arange(NA_KROWS * GRID_W)
    qr_l, qc = qi // GRID_W, qi % GRID_W
    kr_l, kcol = ki // GRID_W, ki % GRID_W
    cstart = np.clip(qc - kc // 2, 0, GRID_W - kc)
    for b in range(nblk):
        r0 = b * NA_QROWS
        kr0 = int(np.clip(r0 - kr // 2, 0, rows - NA_KROWS))
        rstart = np.clip(r0 + np.arange(NA_QROWS) - kr // 2, 0, rows - kr)
        key = (r0 - kr0, tuple(int(t) for t in rstart - kr0))
        if key not in cases:
            cases[key] = len(cases)
            q_row = r0 + qr_l
            k_row = kr0 + kr_l
            rs = rstart[qr_l]
            in_row = (k_row[None, :] >= rs[:, None]) & (k_row[None, :] < rs[:, None] + kr)
            in_col = (kcol[None, :] >= cstart[:, None]) & (kcol[None, :] < cstart[:, None] + kc)
            ir = np.clip(k_row[None, :] - q_row[:, None] + NA_ROWS - 1, 0, 2 * NA_ROWS - 2)
            ic = np.clip(kcol[None, :] - qc[:, None] + NA_COLS - 1, 0, 2 * NA_COLS - 2)
            tables.append((in_row & in_col, ir, ic))
        kr0s.append(kr0)
        case_ids.append(cases[key])
    mask = np.stack([t[0] for t in tables])
    ir = np.stack([t[1] for t in tables])
    ic = np.stack([t[2] for t in tables])
    return np.asarray(kr0s, np.int32), np.asarray(case_ids, np.int32), mask, ir, ic


def _na_bias(rpb, mask, ir, ic):
    n_layers, n_heads = rpb.shape[:2]
    n_case = ir.shape[0]
    ir_s = ir[:, ::GRID_W, ::GRID_W].reshape(-1)
    ic_s = ic[0, :GRID_W, :GRID_W].reshape(-1)
    oh_r = jnp.asarray(np.eye(2 * NA_ROWS - 1, dtype=np.float32)[ir_s])
    oh_c = jnp.asarray(np.eye(2 * NA_COLS - 1, dtype=np.float32)[ic_s].T)
    b = jnp.einsum("xa,lhab,bp->lhxp", oh_r, rpb.astype(F32), oh_c,
                   precision=lax.Precision.HIGHEST)
    b = b.reshape(n_layers, n_heads, n_case, NA_QROWS, NA_KROWS, GRID_W, GRID_W)
    b = b.transpose(0, 2, 1, 3, 5, 4, 6).reshape(
        n_layers, n_case, n_heads, NA_QROWS * GRID_W, NA_KROWS * GRID_W)
    return jnp.where(mask[None, :, None], b, NEG_INF)


def _na_attention(main3, bias, kr0s, case_ids):
    n_seq, seq_len, _ = main3.shape
    tq = NA_QROWS * GRID_W * NA_SUB_BLOCKS
    grid = (n_seq, seq_len // tq)
    return pl.pallas_call(
        _na_kernel,
        grid_spec=pltpu.PrefetchScalarGridSpec(
            num_scalar_prefetch=2,
            grid=grid,
            in_specs=[
                pl.BlockSpec((1, tq, CHUNK), lambda s, i, kr, cs: (s, i, 0)),
                pl.BlockSpec((1, seq_len, CHUNK), lambda s, i, kr, cs: (s, 0, 1)),
                pl.BlockSpec((1, seq_len, CHUNK), lambda s, i, kr, cs: (s, 0, 2)),
                pl.BlockSpec(bias.shape, lambda s, i, kr, cs: (0, 0, 0, 0),
                             pipeline_mode=pl.Buffered(1)),
            ],
            out_specs=pl.BlockSpec((1, tq, CHUNK), lambda s, i, kr, cs: (s, i, 0)),
        ),
        out_shape=jax.ShapeDtypeStruct((n_seq, seq_len, CHUNK), BF16),
        compiler_params=_cparams(2),
        name="mixer_a",
    )(jnp.asarray(kr0s), jnp.asarray(case_ids), main3, main3, main3, bias)


def _band_kernel(*refs, window, n_kv, rep, length, has_sink, want_lse):
    refs = list(refs)
    sink_ref = refs.pop(0) if has_sink else None
    q_ref, k_ref, v_ref = refs[:3]
    o_ref = refs[3]
    lse_ref = refs[4] if want_lse else None
    tq = 2 * window
    kw = tq + 2 * window
    i = pl.program_id(1)
    for sb in range(q_ref.shape[1] // tq):
        rows = slice(sb * tq, (sb + 1) * tq)
        q0 = i * q_ref.shape[1] + sb * tq
        ks = pl.multiple_of(jnp.clip(q0 - window, 0, length - kw), 64)
        q = q_ref[0, rows, :]
        k = k_ref[0, pl.ds(ks, kw), :]
        v = v_ref[0, pl.ds(ks, kw), :]
        qpos = q0 + lax.broadcasted_iota(I32, (tq, kw), 0)
        kpos = ks + lax.broadcasted_iota(I32, (tq, kw), 1)
        valid = jnp.abs(qpos - kpos) <= window
        outs, lses = [], []
        for h in range(n_kv * rep):
            g = h // rep
            qs = slice(h * HEAD_DIM, (h + 1) * HEAD_DIM)
            kvs = slice(g * HEAD_DIM, (g + 1) * HEAD_DIM)
            s = lax.dot_general(q[:, qs], k[:, kvs], (((1,), (1,)), ((), ())),
                                preferred_element_type=F32)
            s = jnp.where(valid, s, NEG_INF)
            m = jnp.max(s, axis=-1, keepdims=True)
            if has_sink:
                m = jnp.maximum(m, sink_ref[h])
            e = jnp.exp(s - m)
            den = jnp.sum(e, axis=-1, keepdims=True)
            if has_sink:
                den = den + jnp.exp(sink_ref[h] - m)
            o = jnp.dot(e.astype(BF16), v[:, kvs], preferred_element_type=F32)
            outs.append(o / den)
            if want_lse:
                lses.append(jnp.broadcast_to(m + jnp.log(den), (tq, HEAD_DIM)))
        o_ref[0, rows, :] = jnp.concatenate(outs, axis=-1).astype(BF16)
        if want_lse:
            lse_ref[0, rows, :] = jnp.concatenate(lses, axis=-1)


def _band_attention(src3, q_blk, k_blk, v_blk, kv_width, window, n_kv, rep, sink=None,
                    want_lse=False):
    n_seq, length, _ = src3.shape
    tq = min(BAND_STEP_ROWS, length)
    qw = n_kv * rep * HEAD_DIM
    kern = functools.partial(_band_kernel, window=window, n_kv=n_kv, rep=rep, length=length,
                             has_sink=sink is not None, want_lse=want_lse)
    in_specs = [
        pl.BlockSpec((1, tq, qw), lambda s, i: (s, i, q_blk)),
        pl.BlockSpec((1, length, kv_width), lambda s, i: (s, 0, k_blk)),
        pl.BlockSpec((1, length, kv_width), lambda s, i: (s, 0, v_blk)),
    ]
    args = [src3, src3, src3]
    if sink is not None:
        in_specs = [pl.BlockSpec(memory_space=pltpu.SMEM)] + in_specs
        args = [sink.astype(F32)] + args
    o_spec = pl.BlockSpec((1, tq, qw), lambda s, i: (s, i, 0))
    out_specs = [o_spec]
    out_shape = [jax.ShapeDtypeStruct((n_seq, length, qw), BF16)]
    if want_lse:
        out_specs.append(o_spec)
        out_shape.append(jax.ShapeDtypeStruct((n_seq, length, qw), F32))
    res = pl.pallas_call(
        kern,
        grid=(n_seq, length // tq),
        in_specs=in_specs,
        out_specs=out_specs,
        out_shape=out_shape,
        compiler_params=_cparams(2),
        name="mixer_band",
    )(*args)
    return res


def _diff_kernel(lam_ref, q_ref, kt_ref, vx_ref, g_ref, o_ref, q_scr, m_scr, acc_scr, *, seq_len):
    tk = min(DIFF_TK, seq_len)
    n_sub = 2 * DIFF_HEADS
    q = q_ref[0]
    for j in range(n_sub):
        q_scr[j] = q[:, j * DIFF_QK_DIM:(j + 1) * DIFF_QK_DIM]
    acc_scr[...] = jnp.zeros(acc_scr.shape, F32)
    lam = lam_ref[0]
    post = lam_ref[1]
    bound = lam_ref[2]

    def operands(c, j):
        c0 = pl.multiple_of(c * tk, tk)
        kt = kt_ref[0, j * DIFF_QK_DIM:(j + 1) * DIFF_QK_DIM, pl.ds(c0, tk)]
        vv = vx_ref[0, pl.ds(c0, tk), (j // 2) * V7X_LANES:(j // 2 + 1) * V7X_LANES]
        return kt, vv

    @pl.when(bound <= DIFF_FAST_BOUND)
    def _():
        def body(c, carry):
            for j in range(n_sub):
                kt, vv = operands(c, j)
                s = jnp.dot(q_scr[j], kt, preferred_element_type=F32)
                p = jnp.exp2(s - bound).astype(BF16)
                acc_scr[j] += jnp.dot(p, vv, preferred_element_type=F32)
            return carry

        lax.fori_loop(0, seq_len // tk, body, 0)

    @pl.when(bound > DIFF_FAST_BOUND)
    def _():
        m_scr[...] = jnp.full(m_scr.shape, NEG_INF, F32)

        def body(c, carry):
            for j in range(n_sub):
                kt, vv = operands(c, j)
                s = jnp.dot(q_scr[j], kt, preferred_element_type=F32)
                mi = m_scr[j]
                mn = jnp.maximum(mi, jnp.max(s, axis=-1, keepdims=True))
                p = jnp.exp2(s - mn)
                alpha = jnp.exp2(mi - mn)
                acc_scr[j] = alpha * acc_scr[j] + jnp.dot(p.astype(BF16), vv,
                                                          preferred_element_type=F32)
                m_scr[j] = mn
            return carry

        lax.fori_loop(0, seq_len // tk, body, 0)

    outs = []
    for h in range(DIFF_HEADS):
        a0 = acc_scr[2 * h]
        a1 = acc_scr[2 * h + 1]
        od = a0[:, :HEAD_DIM] / a0[:, HEAD_DIM:] - lam * (a1[:, :HEAD_DIM] / a1[:, HEAD_DIM:])
        ms = jnp.mean(od * od, axis=-1, keepdims=True)
        outs.append(od * lax.rsqrt(ms + NORM_EPS) * g_ref[...] * post)
    o_ref[0] = jnp.concatenate(outs, axis=-1).astype(BF16)


def _diff_attention(main3, kdt, vdx3, lam_post, subln):
    n_seq, seq_len, _ = main3.shape
    tq = DIFF_TQ
    kern = functools.partial(_diff_kernel, seq_len=seq_len)
    return pl.pallas_call(
        kern,
        grid=(n_seq, seq_len // tq),
        in_specs=[
            pl.BlockSpec(memory_space=pltpu.SMEM),
            pl.BlockSpec((1, tq, CHUNK), lambda s, i: (s, i, MAIN_COLS // CHUNK - 1)),
            pl.BlockSpec((1, CHUNK, seq_len), lambda s, i: (s, 0, 0),
                         pipeline_mode=pl.Buffered(1)),
            pl.BlockSpec((1, seq_len, 2 * CHUNK), lambda s, i: (s, 0, 0),
                         pipeline_mode=pl.Buffered(1)),
            pl.BlockSpec((1, HEAD_DIM), lambda s, i: (0, 0)),
        ],
        out_specs=pl.BlockSpec((1, tq, CHUNK), lambda s, i: (s, i, 0)),
        out_shape=jax.ShapeDtypeStruct((n_seq, seq_len, CHUNK), BF16),
        scratch_shapes=[
            pltpu.VMEM((2 * DIFF_HEADS, tq, DIFF_QK_DIM), BF16),
            pltpu.VMEM((2 * DIFF_HEADS, tq, 1), F32),
            pltpu.VMEM((2 * DIFF_HEADS, tq, V7X_LANES), F32),
        ],
        compiler_params=_cparams(2),
        name="mixer_d",
    )(lam_post, main3, kdt, vdx3, subln)


def _out_proj_kernel(h_ref, oa_ref, ob_ref, oc0_ref, oc1_ref, oc2_ref, l0_ref, l1_ref, l2_ref,
                     od_ref, w_ref, fn_ref, wr_ref, h1_ref, hn_ref, aff_ref, *scratch):
    tm = h_ref.shape[0]

    def token_major(ref, scr):
        dil = ref.shape[1]
        halves = []
        for half_i in range(CHUNK // V7X_LANES):
            lanes = slice(half_i * V7X_LANES, (half_i + 1) * V7X_LANES)
            for r in range(dil):
                scr.at[half_i][pl.ds(r, tm // dil, stride=dil), :] = ref[0, r, :, lanes].astype(F32)
            halves.append(scr[half_i])
        return jnp.concatenate(halves, axis=-1)

    l0 = l0_ref[...]
    l1 = token_major(l1_ref, scratch[0])
    l2 = token_major(l2_ref, scratch[1])
    mx = jnp.maximum(jnp.maximum(l0, l1), l2)
    e0, e1, e2 = jnp.exp(l0 - mx), jnp.exp(l1 - mx), jnp.exp(l2 - mx)
    den = e0 + e1 + e2
    oc = ((e0 / den) * oc0_ref[...].astype(F32) + (e1 / den) * token_major(oc1_ref, scratch[2])
          + (e2 / den) * token_major(oc2_ref, scratch[3]))
    acc = h_ref[...]
    acc = acc + jnp.dot(oa_ref[...], w_ref[0 * CHUNK:1 * CHUNK, :], preferred_element_type=F32)
    acc = acc + jnp.dot(ob_ref[...], w_ref[1 * CHUNK:2 * CHUNK, :], preferred_element_type=F32)
    acc = acc + jnp.dot(oc.astype(BF16), w_ref[2 * CHUNK:3 * CHUNK, :], preferred_element_type=F32)
    acc = acc + jnp.dot(od_ref[...], w_ref[3 * CHUNK:4 * CHUNK, :], preferred_element_type=F32)
    h1_ref[...] = acc
    ms = jnp.mean(acc * acc, axis=-1, keepdims=True)
    hn = (acc * lax.rsqrt(ms + NORM_EPS) * fn_ref[...]).astype(BF16)
    hn_ref[...] = hn
    logits = jnp.dot(hn, wr_ref[...], preferred_element_type=F32)
    m = jnp.max(logits, axis=-1, keepdims=True)
    e = jnp.exp(logits - m)
    aff_ref[...] = e / jnp.sum(e, axis=-1, keepdims=True)


def _out_proj(h2, oa, ob, ocs, lses, od, w_out, ffn_norm, w_router, tm, seq_len):
    n = h2.shape[0]
    nblk = seq_len // tm
    row = lambda w: pl.BlockSpec((tm, w), lambda i: (i, 0))
    const = lambda r, c: pl.BlockSpec((r, c), lambda i: (0, 0))
    dil = lambda g: _dilated_spec(DIL_CONFIGS[g][1], tm, CHUNK, nblk)
    return pl.pallas_call(
        _out_proj_kernel,
        grid=(n // tm,),
        in_specs=[row(D_MODEL), row(CHUNK), row(CHUNK), row(CHUNK), dil(1), dil(2),
                  row(CHUNK), dil(1), dil(2), row(CHUNK),
                  const(D_MODEL, D_MODEL), const(1, D_MODEL), const(D_MODEL, N_EXPERTS)],
        out_specs=[row(D_MODEL), row(D_MODEL), row(N_EXPERTS)],
        out_shape=[
            jax.ShapeDtypeStruct((n, D_MODEL), F32),
            jax.ShapeDtypeStruct((n, D_MODEL), BF16),
            jax.ShapeDtypeStruct((n, N_EXPERTS), F32),
        ],
        scratch_shapes=[pltpu.VMEM((CHUNK // V7X_LANES, tm, V7X_LANES), F32)] * 4,
        compiler_params=_cparams(1),
        name="out_proj",
    )(h2, oa, ob, ocs[0], ocs[1], ocs[2], lses[0], lses[1], lses[2], od, w_out, ffn_norm,
      w_router)


def _select_kernel(aff_ref, posm_ref, pref_ref, *, cap, n_tok):
    def bits():
        return lax.bitcast_convert_type(aff_ref[...], I32)

    def bit_body(i, thr):
        cand = thr | jnp.left_shift(jnp.int32(1), 30 - i)
        cnt = jnp.sum(jnp.where(bits() >= cand, 1.0, 0.0), axis=-1, keepdims=True)
        return jnp.where(cnt >= cap, cand, thr)

    thr = lax.fori_loop(0, 31, bit_body, jnp.zeros((N_EXPERTS, 1), I32))
    cnt_gt = jnp.sum(jnp.where(bits() > thr, 1.0, 0.0), axis=-1, keepdims=True)
    need = cap - cnt_gt
    blk = SLOT_BLK
    r = lax.broadcasted_iota(I32, (blk, blk), 0)
    c = lax.broadcasted_iota(I32, (blk, blk), 1)
    upper = jnp.where(r < c, 1.0, 0.0).astype(BF16)

    def blk_body(b, carry):
        ceq, csel = carry
        b0 = pl.multiple_of(b * blk, blk)
        xb = lax.bitcast_convert_type(aff_ref[:, pl.ds(b0, blk)], I32)
        gt = xb > thr
        eq = xb == thr
        eqf = jnp.where(eq, 1.0, 0.0)
        rank = ceq + jnp.dot(eqf.astype(BF16), upper, preferred_element_type=F32)
        sel = gt | (eq & (rank < need))
        self_ = jnp.where(sel, 1.0, 0.0)
        pref = csel + jnp.dot(self_.astype(BF16), upper, preferred_element_type=F32)
        prefi = pref.astype(I32)
        pref_ref[:, pl.ds(b0, blk)] = prefi
        posm_ref[:, pl.ds(b0, blk)] = jnp.where(sel, prefi, -1)
        return (ceq + jnp.sum(eqf, axis=-1, keepdims=True),
                csel + jnp.sum(self_, axis=-1, keepdims=True))

    zero = jnp.zeros((N_EXPERTS, 1), F32)
    lax.fori_loop(0, n_tok // blk, blk_body, (zero, zero))


def _select(aff_t, cap):
    n_tok = aff_t.shape[1]
    kern = functools.partial(_select_kernel, cap=cap, n_tok=n_tok)
    full = pl.BlockSpec((N_EXPERTS, n_tok), lambda i: (0, 0))
    return pl.pallas_call(
        kern,
        grid=(1,),
        in_specs=[full],
        out_specs=[full, full],
        out_shape=[jax.ShapeDtypeStruct((N_EXPERTS, n_tok), I32)] * 2,
        compiler_params=_cparams(1),
        name="select",
    )(aff_t)


_VALID, _FIRST, _LAST = 1, 2, 4


_NO_HIT = 1 << 30


def _block_windows(pref, cap, tok_blk):
    n_exp = pref.shape[0]
    nw = cap // SLOT_BLK
    off = pref[:, ::tok_blk]
    end = jnp.concatenate([off[:, 1:], jnp.full((n_exp, 1), cap, I32)], axis=1)
    cnt = end - off
    w_lo = jnp.minimum(off // SLOT_BLK, nw - 1)
    w_hi = jnp.where(cnt > 0, (end - 1) // SLOT_BLK, w_lo)
    return cnt, w_lo, w_hi


def _search(starts, idx):
    return (jnp.sum(starts[None, :] <= idx.reshape(-1, 1), axis=1) - 1).astype(I32).reshape(idx.shape)


def _gather_plan(pref, cap):
    n_exp, n_tok = pref.shape
    nb = n_tok // GATHER_TOK_BLK
    nw = cap // SLOT_BLK
    _, w_lo, w_hi = _block_windows(pref, cap, GATHER_TOK_BLK)
    s_g = nb + nw
    nst = w_hi - w_lo + 1
    start = jnp.cumsum(nst, axis=1) - nst
    total = jnp.sum(nst, axis=1, keepdims=True)
    s_idx = jnp.arange(s_g, dtype=I32)[None, :]
    s_eff = jnp.minimum(s_idx, total - 1)
    b_s = jax.vmap(_search)(start, s_eff)
    w_s = jnp.take_along_axis(w_lo, b_s, axis=1) + (s_eff - jnp.take_along_axis(start, b_s, axis=1))
    valid = s_idx < total
    prev_w = jnp.concatenate([jnp.full((n_exp, 1), -1, I32), w_s[:, :-1]], axis=1)
    next_w = jnp.concatenate([w_s[:, 1:], jnp.full((n_exp, 1), -1, I32)], axis=1)
    first = valid & (w_s != prev_w)
    last = valid & ((w_s != next_w) | (s_idx == total - 1))
    flag = valid * _VALID + first * _FIRST + last * _LAST
    return (b_s.reshape(-1), w_s.reshape(-1).astype(I32), flag.reshape(-1).astype(I32)), s_g


def _combine_plan(pref, cap):
    n_exp, n_tok = pref.shape
    nb = n_tok // COMBINE_TOK_BLK
    nw = cap // SLOT_BLK
    grp = COMBINE_GROUP
    cnt, w_lo, w_hi = _block_windows(pref, cap, COMBINE_TOK_BLK)
    cnt_t, wlo_t, whi_t = cnt.T, w_lo.T, w_hi.T
    n_win = jnp.where(cnt_t > 0, whi_t - wlo_t + 1, 0)
    win_start = (jnp.cumsum(n_win.reshape(-1)) - n_win.reshape(-1)).astype(I32)
    per_blk = jnp.sum(n_win, axis=1)
    blk_first_win = (jnp.cumsum(per_blk) - per_blk).astype(I32)
    steps = jnp.maximum((per_blk + grp - 1) // grp, 1)
    step_start = (jnp.cumsum(steps) - steps).astype(I32)
    total = jnp.sum(steps)
    s_c = (n_exp * (nb + nw) + grp - 1) // grp + nb
    s_idx = jnp.arange(s_c, dtype=I32)
    s_eff = jnp.minimum(s_idx, total - 1)
    cb = _search(step_start, s_eff)
    local = s_eff - step_start[cb]
    q = local[:, None] * grp + jnp.arange(grp, dtype=I32)[None, :]
    live = (q < per_blk[cb][:, None]) & (s_idx < total)[:, None]
    win = jnp.where(live, blk_first_win[cb][:, None] + q, 0)
    pair = _search(win_start, win.reshape(-1)).reshape(win.shape)
    ce = pair % n_exp
    cw = wlo_t.reshape(-1)[pair] + (win - win_start[pair])
    cw = jnp.clip(cw, 0, nw - 1)
    blk = (ce * nw + cw).astype(I32)
    base = jnp.where(live, cw * SLOT_BLK, _NO_HIT).astype(I32)
    valid = s_idx < total
    prev_b = jnp.concatenate([jnp.full((1,), -1, I32), cb[:-1]])
    next_b = jnp.concatenate([cb[1:], jnp.full((1,), -1, I32)])
    first = valid & (cb != prev_b)
    last = valid & ((cb != next_b) | (s_idx == total - 1))
    flag = (valid * _VALID + first * _FIRST + last * _LAST).astype(I32)
    return (cb, ce.reshape(-1).astype(I32), blk.reshape(-1), base.reshape(-1), flag), s_c


def _gather_kernel(gb_ref, gw_ref, gf_ref, posm_ref, g_ref, hn_ref, xe_ref, gs_ref, acc_ref,
                   gacc_ref, *, n_steps):
    e = pl.program_id(0)
    s = pl.program_id(1)
    flag = gf_ref[e * n_steps + s]
    w = gw_ref[e * n_steps + s]

    @pl.when((flag & _FIRST) != 0)
    def _():
        acc_ref[...] = jnp.zeros_like(acc_ref)
        gacc_ref[...] = jnp.zeros_like(gacc_ref)

    @pl.when((flag & _VALID) != 0)
    def _():
        rel = posm_ref[0] - w * SLOT_BLK
        hit = lax.broadcasted_iota(I32, (SLOT_BLK, GATHER_TOK_BLK), 0) == rel
        onehot = jnp.where(hit, 1.0, 0.0).astype(BF16)
        acc_ref[...] += jnp.dot(onehot, hn_ref[...], preferred_element_type=F32)
        gacc_ref[...] += jnp.sum(jnp.where(hit, g_ref[0], 0.0), axis=-1, keepdims=True)

    @pl.when((flag & _LAST) != 0)
    def _():
        xe_ref[0] = acc_ref[...].astype(BF16)
        gs_ref[0] = gacc_ref[...]


def _gather(plan, s_g, posm3, g3, hn, cap):
    gb, gw, gf = plan
    n_tok = hn.shape[0]
    kern = functools.partial(_gather_kernel, n_steps=s_g)
    tok = lambda e, s, gb, gw, gf: (e, 0, gb[e * s_g + s])
    slot = lambda e, s, gb, gw, gf: (e, gw[e * s_g + s], 0)
    return pl.pallas_call(
        kern,
        grid_spec=pltpu.PrefetchScalarGridSpec(
            num_scalar_prefetch=3,
            grid=(N_EXPERTS, s_g),
            in_specs=[
                pl.BlockSpec((1, 1, GATHER_TOK_BLK), tok),
                pl.BlockSpec((1, 1, GATHER_TOK_BLK), tok),
                pl.BlockSpec((GATHER_TOK_BLK, D_MODEL), lambda e, s, gb, gw, gf: (gb[e * s_g + s], 0)),
            ],
            out_specs=[
                pl.BlockSpec((1, SLOT_BLK, D_MODEL), slot),
                pl.BlockSpec((1, SLOT_BLK, 1), slot),
            ],
            scratch_shapes=[pltpu.VMEM((SLOT_BLK, D_MODEL), F32), pltpu.VMEM((SLOT_BLK, 1), F32)],
        ),
        out_shape=[
            jax.ShapeDtypeStruct((N_EXPERTS, cap, D_MODEL), BF16),
            jax.ShapeDtypeStruct((N_EXPERTS, cap, 1), F32),
        ],
        compiler_params=_cparams(2),
        name="gather",
    )(gb, gw, gf, posm3, g3, hn)


def _experts_kernel(x_ref, gs_ref, wg_ref, wu_ref, wd_ref, y_ref):
    x = x_ref[0]
    hg = jnp.dot(x, wg_ref[0], preferred_element_type=F32)
    hu = jnp.dot(x, wu_ref[0], preferred_element_type=F32)
    hdn = (hg / (1.0 + jnp.exp(-hg)) * hu).astype(BF16)
    y = jnp.dot(hdn, wd_ref[0], preferred_element_type=F32)
    y_ref[0] = (y * gs_ref[0]).astype(BF16)


def _experts(xe, gs, w_gate, w_up, w_down):
    n_exp, cap, _ = xe.shape
    tc = min(512, cap)
    wspec = pl.BlockSpec((1, D_MODEL, D_MODEL), lambda e, c: (e, 0, 0))
    return pl.pallas_call(
        _experts_kernel,
        grid=(n_exp, cap // tc),
        in_specs=[
            pl.BlockSpec((1, tc, D_MODEL), lambda e, c: (e, c, 0)),
            pl.BlockSpec((1, tc, 1), lambda e, c: (e, c, 0)),
            wspec, wspec, wspec,
        ],
        out_specs=pl.BlockSpec((1, tc, D_MODEL), lambda e, c: (e, c, 0)),
        out_shape=jax.ShapeDtypeStruct((n_exp, cap, D_MODEL), BF16),
        compiler_params=_cparams(2),
        name="experts",
    )(xe, gs, w_gate, w_up, w_down)


def _combine_kernel(cb_ref, ce_ref, cblk_ref, cbase_ref, cf_ref, h1_ref, posmt_ref, *rest):
    del cb_ref, cblk_ref
    grp = COMBINE_GROUP
    ye_refs = rest[:grp]
    p_ref, pn_ref, wpg_ref, wpp_ref, o_ref, acc_ref = rest[grp:]
    s = pl.program_id(0)
    flag = cf_ref[s]

    @pl.when((flag & _FIRST) != 0)
    def _():
        acc_ref[...] = jnp.zeros_like(acc_ref)

    @pl.when((flag & _VALID) != 0)
    def _():
        pm = posmt_ref[...].astype(F32)
        lane = lax.broadcasted_iota(I32, pm.shape, 1)
        slot_iota = lax.broadcasted_iota(I32, (COMBINE_TOK_BLK, SLOT_BLK), 1)
        total = None
        for k in range(grp):
            e = ce_ref[s * grp + k]
            col = jnp.sum(jnp.where(lane == e, pm, 0.0), axis=-1, keepdims=True)
            rel = col.astype(I32) - cbase_ref[s * grp + k]
            onehot = jnp.where(slot_iota == rel, 1.0, 0.0).astype(BF16)
            part = jnp.dot(onehot, ye_refs[k][0], preferred_element_type=F32)
            total = part if total is None else total + part
        acc_ref[...] += total

    @pl.when((flag & _LAST) != 0)
    def _():
        h2 = h1_ref[...] + acc_ref[...]
        ms = jnp.mean(h2 * h2, axis=-1, keepdims=True)
        hn = (h2 * lax.rsqrt(ms + NORM_EPS) * pn_ref[...]).astype(BF16)
        z = jnp.dot(hn, wpg_ref[...], preferred_element_type=F32)
        gate = 1.0 / (1.0 + jnp.exp(-z))
        proj = jnp.dot(p_ref[...].astype(BF16), wpp_ref[...], preferred_element_type=F32)
        o_ref[...] = h2 + gate * proj


def _combine(plan, s_c, h1, posm_t, ye, p2, ple_norm, w_ple_gate, w_ple_proj, cap):
    cb, ce, cblk, cbase, cf = plan
    n_tok = h1.shape[0]
    nw = cap // SLOT_BLK
    grp = COMBINE_GROUP
    ye3 = ye.reshape(N_EXPERTS * nw, SLOT_BLK, D_MODEL)
    tokb = lambda width: pl.BlockSpec((COMBINE_TOK_BLK, width), lambda s, cb, *_: (cb[s], 0))
    const = lambda r, c: pl.BlockSpec((r, c), lambda s, *_: (0, 0))

    def ye_spec(k):
        return pl.BlockSpec((1, SLOT_BLK, D_MODEL),
                            lambda s, cb, ce, cblk, *_: (cblk[s * grp + k], 0, 0))

    return pl.pallas_call(
        _combine_kernel,
        grid_spec=pltpu.PrefetchScalarGridSpec(
            num_scalar_prefetch=5,
            grid=(s_c,),
            in_specs=[tokb(D_MODEL), tokb(N_EXPERTS)] + [ye_spec(k) for k in range(grp)] + [
                tokb(PLE_DIM), const(1, D_MODEL), const(D_MODEL, D_MODEL),
                const(PLE_DIM, D_MODEL)],
            out_specs=tokb(D_MODEL),
            scratch_shapes=[pltpu.VMEM((COMBINE_TOK_BLK, D_MODEL), F32)],
        ),
        out_shape=jax.ShapeDtypeStruct((n_tok, D_MODEL), F32),
        compiler_params=_cparams(1),
        name="combine",
    )(cb, ce, cblk, cbase, cf, h1, posm_t, *([ye3] * grp), p2, ple_norm, w_ple_gate, w_ple_proj)


def _layer(h2, p2, lw, static):
    n_seq, seq_len, tabs, b64, b32, na_plan = static
    n_tok = h2.shape[0]
    tm = min(512, seq_len)
    gains = _in_gains(lw["qk_norm_a"], lw["qk_norm_b"], lw["qk_norm_c"], lw["qk_norm_d"])
    main, cg1, cg2, kdt, vdx = _in_proj(h2, lw["attn_norm"][None, :], lw["w_in"], b64, b32, gains,
                                        tabs, n_seq, seq_len, tm)
    main3 = main.reshape(n_seq, seq_len, MAIN_COLS)

    kr0s, case_ids = na_plan
    oa = _na_attention(main3, lw["na_bias"], kr0s, case_ids)

    (ob,) = _band_attention(main3, 3, 8, 9, V7X_LANES, SW_WINDOW, SW_KV_HEADS,
                            SW_HEADS // SW_KV_HEADS, sink=lw["sink_b"])
    ocs, lses = [], []
    for g_i, (win, dil) in enumerate(DIL_CONFIGS):
        if dil == 1:
            src, cols = main3, (5, 6, 7)
        else:
            src = (cg1, cg2)[g_i - 1].reshape(n_seq * dil, seq_len // dil, 3 * CHUNK)
            cols = (0, 1, 2)
        o_g, l_g = _band_attention(src, *cols, CHUNK, win // (2 * dil), DIL_HEADS, 1,
                                   want_lse=True)
        shape = (n_tok, CHUNK) if dil == 1 else (n_seq, dil, seq_len // dil, CHUNK)
        ocs.append(o_g.reshape(shape))
        lses.append(l_g.reshape(shape))

    lf = lw["lambda_d"].astype(F32)
    lam = jnp.exp(jnp.sum(lf[0] * lf[1])) - jnp.exp(jnp.sum(lf[2] * lf[3])) + lw["lam_init"]
    gd = jnp.abs(lw["qk_norm_d"].astype(F32))
    score_bound = (LOG2_E * DIFF_QK_DIM ** 0.5 * (1.0 + 2.0 ** -6)) * jnp.max(gd[0]) * jnp.max(gd[1])
    lam_post = jnp.stack([lam, 1.0 - lw["lam_init"], score_bound]).astype(F32)
    od = _diff_attention(main3, kdt, vdx.reshape(n_seq, seq_len, 2 * CHUNK), lam_post,
                         lw["subln_d"].astype(F32)[None, :])

    h1, hn, aff = _out_proj(h2, oa.reshape(n_tok, CHUNK), ob.reshape(n_tok, CHUNK), ocs, lses,
                            od.reshape(n_tok, CHUNK), lw["w_out"], lw["ffn_norm"][None, :],
                            lw["w_router"], tm, seq_len)

    cap = EC_CAPACITY_FACTOR * n_tok // N_EXPERTS
    aff_t = aff.T
    posm, pref = _select(aff_t, cap)
    gather_plan, s_g = _gather_plan(pref, cap)
    combine_plan, s_c = _combine_plan(pref, cap)
    xe, gs = _gather(gather_plan, s_g, posm.reshape(N_EXPERTS, 1, n_tok),
                     aff_t.reshape(N_EXPERTS, 1, n_tok), hn, cap)
    ye = _experts(xe, gs, lw["w_gate"], lw["w_up"], lw["w_down"])
    return _combine(combine_plan, s_c, h1, posm.T, ye, p2, lw["ple_norm"][None, :],
                    lw["w_ple_gate"], lw["w_ple_proj"], cap)


def _trunk(x, p, weights):
    n_seq, seq_len, _ = x.shape
    depth = p.shape[0]
    tabs = _rope_tables(seq_len, HEAD_DIM // 2, 4) + _rope_tables(seq_len, DIFF_QK_DIM // 2, 8)
    kr0s, case_ids, mask, ir, ic = _na_plan(seq_len // GRID_W)
    static = (n_seq, seq_len, tabs, _block_mean_matrix(HEAD_DIM), _block_mean_matrix(DIFF_QK_DIM),
              (kr0s, case_ids))
    xs = dict(weights)
    xs["p"] = p.reshape(depth, n_seq * seq_len, PLE_DIM)
    xs["na_bias"] = _na_bias(weights["rpb_a"], mask, ir, ic)

    def step(h2, lw):
        return _layer(h2, lw["p"], lw, static), None

    h2, _ = lax.scan(step, x.reshape(n_seq * seq_len, D_MODEL), xs)
    return h2.reshape(n_seq, seq_len, D_MODEL)


def kernel(x_prompt, x_sample, p_prompt, p_sample, attn_norm, w_in, rpb_a, qk_norm_a, qk_norm_b, sink_b, qk_norm_c, qk_norm_d, lambda_d, subln_d, w_out, ffn_norm, w_router, w_gate, w_up, w_down, ple_norm, w_ple_gate, w_ple_proj):
    depth = w_in.shape[0]
    lam_init = jnp.asarray([0.8 - 0.6 * math.exp(-0.3 * i) for i in range(depth)], F32)
    weights = dict(
        attn_norm=attn_norm.astype(F32), w_in=jax.vmap(_extend_w_in)(w_in), rpb_a=rpb_a,
        qk_norm_a=qk_norm_a, qk_norm_b=qk_norm_b, sink_b=sink_b, qk_norm_c=qk_norm_c,
        qk_norm_d=qk_norm_d, lambda_d=lambda_d, subln_d=subln_d, w_out=w_out.astype(BF16),
        ffn_norm=ffn_norm.astype(F32), w_router=w_router.astype(BF16), w_gate=w_gate.astype(BF16),
        w_up=w_up.astype(BF16), w_down=w_down.astype(BF16), ple_norm=ple_norm.astype(F32),
        w_ple_gate=w_ple_gate.astype(BF16), w_ple_proj=w_ple_proj.astype(BF16), lam_init=lam_init)
    y_prompt = _trunk(x_prompt, p_prompt, weights)
    y_sample = _trunk(x_sample, p_sample, weights)
    return (y_prompt, y_sample)
```

```python
import functools
import math

import jax
import jax.numpy as jnp
import numpy as np
from jax import lax
from jax.experimental import pallas as pl
from jax.experimental.pallas import tpu as pltpu

F32 = jnp.float32
BF16 = jnp.bfloat16
I32 = jnp.int32

D_MODEL = 1024
HEAD_DIM = 64
GRID_W = 64
NA_HEADS = 4
NA_ROWS = 8
NA_COLS = 16
SW_HEADS = 4
SW_KV_HEADS = 2
SW_WINDOW = 128
DIL_CONFIGS = ((128, 1), (512, 4), (2048, 16))
DIL_HEADS = 4
DIFF_HEADS = 4
DIFF_QK_DIM = HEAD_DIM // 2
ROPE_THETA = 10000.0
N_EXPERTS = 16
EC_CAPACITY_FACTOR = 2
PLE_DIM = 256
NEG_INF = -1e30
NORM_EPS = 1e-6

V7X_LANES = 128
V7X_VMEM_LIMIT = 56 * 1024 * 1024

CHUNK = 256
MAIN_COLS = 9 * CHUNK
N_IN_CHUNKS = 18
NA_QROWS = 2
NA_KROWS = NA_QROWS + NA_ROWS - 1
NA_SUB_BLOCKS = 4
BAND_STEP_ROWS = 512
DIFF_TQ = 1024
DIFF_TK = 512
DIFF_FAST_BOUND = 40.0
LOG2_E = math.log2(math.e)
SLOT_BLK = 256
GATHER_TOK_BLK = 2048
COMBINE_TOK_BLK = 512
COMBINE_SLOT_BLK = 128
COMBINE_GROUP = 8


def _cparams(n_axes):
    return pltpu.CompilerParams(
        dimension_semantics=("arbitrary",) * n_axes, vmem_limit_bytes=V7X_VMEM_LIMIT)


_IN_CHUNK_KIND = (
    (64, 0), (64, 0), (0, 0),
    (64, 32), (64, 32),
    (64, 32), (64, 32), (64, 32),
    (64, 32), (64, 32), (64, 32),
    (0, 0), (0, 0), (0, 0),
    (32, 16),
    (32, 16),
    (0, 0), (0, 0),
)
_KBVB_CHUNK = 4
_IN_CHUNK_DEST = (
    ("main", 0), ("main", 1), ("main", 2), ("main", 3), ("main", 4),
    ("main", 5), ("dil", (0, 0)), ("dil", (1, 0)),
    ("main", 6), ("dil", (0, 1)), ("dil", (1, 1)),
    ("main", 7), ("dil", (0, 2)), ("dil", (1, 2)),
    ("main", 8), ("kdt", 0), ("vdx", 0), ("vdx", 1),
)


def _in_proj_kernel(h_ref, gn_ref, w_ref, b64_ref, b32_ref, gains_ref, cs64_ref, sn64_ref,
                    cs32_ref, sn32_ref, main_ref, cg1_ref, cg2_ref, kdt_ref, vdx_ref, row_scr):
    x = h_ref[...]
    ms = jnp.mean(x * x, axis=-1, keepdims=True)
    xn = (x * lax.rsqrt(ms + NORM_EPS) * gn_ref[...]).astype(BF16)
    tm = x.shape[0]
    lane = lax.broadcasted_iota(I32, (tm, CHUNK), 1)
    for c in range(N_IN_CHUNKS):
        hd, half = _IN_CHUNK_KIND[c]
        y = jnp.dot(xn, w_ref[:, c * CHUNK:(c + 1) * CHUNK], preferred_element_type=F32)
        out = y
        if hd:
            bmat = b64_ref[...] if hd == 64 else b32_ref[...]
            msq = jnp.dot((y * y).astype(BF16), bmat, preferred_element_type=F32)
            out = y * lax.rsqrt(msq + NORM_EPS) * gains_ref[c:c + 1, :]
        if half:
            cs = cs64_ref[...] if half == 32 else cs32_ref[...]
            sn = sn64_ref[...] if half == 32 else sn32_ref[...]
            fwd = pltpu.roll(out, half, 1)
            bwd = pltpu.roll(out, CHUNK - half, 1)
            first = (lane & (2 * half - 1)) < half
            out = out * cs + jnp.where(first, bwd, fwd) * sn
        if c == _KBVB_CHUNK:
            out = jnp.where(lane < CHUNK // 2, out, y)
        kind, where = _IN_CHUNK_DEST[c]
        if kind == "main":
            main_ref[:, where * CHUNK:(where + 1) * CHUNK] = out.astype(BF16)
        elif kind == "dil":
            group, part = where
            dst = (cg1_ref, cg2_ref)[group]
            dil = dst.shape[1]
            for half_i in range(CHUNK // V7X_LANES):
                lanes = slice(half_i * V7X_LANES, (half_i + 1) * V7X_LANES)
                row_scr[half_i] = out[:, lanes]
                for r in range(dil):
                    col0 = part * CHUNK + half_i * V7X_LANES
                    dst[0, r, :, col0:col0 + V7X_LANES] = row_scr.at[half_i][
                        pl.ds(r, tm // dil, stride=dil), :].astype(BF16)
        elif kind == "kdt":
            kdt_ref[0] = out.T.astype(BF16)
        else:
            ones = jnp.where((lane & (V7X_LANES - 1)) >= HEAD_DIM, 1.0, 0.0)
            vdx_ref[:, where * CHUNK:(where + 1) * CHUNK] = (out + ones).astype(BF16)


def _dilated_spec(dil, tm, width, nblk):
    return pl.BlockSpec((1, dil, tm // dil, width), lambda i: (i // nblk, 0, i % nblk, 0))


def _in_proj(h2, gn, w_ext, b64, b32, gains, tabs, n_seq, seq_len, tm):
    n = h2.shape[0]
    nblk = seq_len // tm
    d1, d2 = DIL_CONFIGS[1][1], DIL_CONFIGS[2][1]
    cs64, sn64, cs32, sn32 = tabs
    const = lambda i: (0, 0)
    tab = lambda i: (i % nblk, 0)
    return pl.pallas_call(
        _in_proj_kernel,
        grid=(n // tm,),
        in_specs=[
            pl.BlockSpec((tm, D_MODEL), lambda i: (i, 0)),
            pl.BlockSpec((1, D_MODEL), const),
            pl.BlockSpec((D_MODEL, N_IN_CHUNKS * CHUNK), const),
            pl.BlockSpec((CHUNK, CHUNK), const),
            pl.BlockSpec((CHUNK, CHUNK), const),
            pl.BlockSpec((N_IN_CHUNKS, CHUNK), const),
            pl.BlockSpec((tm, CHUNK), tab),
            pl.BlockSpec((tm, CHUNK), tab),
            pl.BlockSpec((tm, CHUNK), tab),
            pl.BlockSpec((tm, CHUNK), tab),
        ],
        out_specs=[
            pl.BlockSpec((tm, MAIN_COLS), lambda i: (i, 0)),
            _dilated_spec(d1, tm, 3 * CHUNK, nblk),
            _dilated_spec(d2, tm, 3 * CHUNK, nblk),
            pl.BlockSpec((1, CHUNK, tm), lambda i: (i // nblk, 0, i % nblk)),
            pl.BlockSpec((tm, 2 * CHUNK), lambda i: (i, 0)),
        ],
        out_shape=[
            jax.ShapeDtypeStruct((n, MAIN_COLS), BF16),
            jax.ShapeDtypeStruct((n_seq, d1, seq_len // d1, 3 * CHUNK), BF16),
            jax.ShapeDtypeStruct((n_seq, d2, seq_len // d2, 3 * CHUNK), BF16),
            jax.ShapeDtypeStruct((n_seq, CHUNK, seq_len), BF16),
            jax.ShapeDtypeStruct((n, 2 * CHUNK), BF16),
        ],
        scratch_shapes=[pltpu.VMEM((CHUNK // V7X_LANES, tm, V7X_LANES), F32)],
        compiler_params=_cparams(1),
        name="in_proj",
    )(h2, gn, w_ext, b64, b32, gains, cs64, sn64, cs32, sn32)


def _rope_tables(seq_len, half, reps):
    inv_freq = ROPE_THETA ** (-jnp.arange(half, dtype=F32) / half)
    ang = jnp.arange(seq_len, dtype=F32)[:, None] * inv_freq[None, :]
    cos, sin = jnp.cos(ang), jnp.sin(ang)
    cs = jnp.tile(jnp.concatenate([cos, cos], axis=-1), (1, reps))
    sn = jnp.tile(jnp.concatenate([-sin, sin], axis=-1), (1, reps))
    return cs, sn


def _block_mean_matrix(head):
    blocks = CHUNK // head
    return jnp.asarray(np.kron(np.eye(blocks), np.full((head, head), 1.0 / head)), BF16)


def _extend_w_in(w_in):
    vd = w_in[:, 4096:4352].reshape(D_MODEL, DIFF_HEADS, HEAD_DIM)
    vdx = jnp.concatenate([vd, jnp.zeros_like(vd)], axis=-1).reshape(D_MODEL, 2 * CHUNK)
    return jnp.concatenate([w_in[:, :4096], vdx], axis=-1).astype(BF16)


def _in_gains(qk_a, qk_b, qk_c, qk_d):
    one = jnp.ones((CHUNK,), F32)
    s64 = HEAD_DIM ** -0.5
    s32 = DIFF_QK_DIM ** -0.5
    t4 = lambda g: jnp.tile(g.astype(F32), 4)
    rows = [
        t4(qk_a[0]) * s64, t4(qk_a[1]), one,
        t4(qk_b[0]) * s64, jnp.concatenate([jnp.tile(qk_b[1].astype(F32), 2), jnp.ones((128,), F32)]),
        t4(qk_c[0]) * s64, t4(qk_c[0]) * s64, t4(qk_c[0]) * s64,
        t4(qk_c[1]), t4(qk_c[1]), t4(qk_c[1]),
        one, one, one,
        jnp.tile(qk_d[0].astype(F32), 8) * (s32 * LOG2_E),
        jnp.tile(qk_d[1].astype(F32), 8),
        one, one,
    ]
    return jnp.stack(rows)


def _na_kernel(kr0_ref, case_ref, q_ref, k_ref, v_ref, bias_ref, o_ref):
    i = pl.program_id(1)
    tq = NA_QROWS * GRID_W
    nk = NA_KROWS * GRID_W
    n_sub = q_ref.shape[1] // tq
    for sb in range(n_sub):
        blk = i * n_sub + sb
        rows = slice(sb * tq, (sb + 1) * tq)
        ks = pl.multiple_of(kr0_ref[blk] * GRID_W, GRID_W)
        case = case_ref[blk]
        q = q_ref[0, rows, :]
        k = k_ref[0, pl.ds(ks, nk), :]
        v = v_ref[0, pl.ds(ks, nk), :]
        outs = []
        for h in range(NA_HEADS):
            sl = slice(h * HEAD_DIM, (h + 1) * HEAD_DIM)
            s = lax.dot_general(q[:, sl], k[:, sl], (((1,), (1,)), ((), ())),
                                preferred_element_type=F32)
            s = s + bias_ref[case, h]
            m = jnp.max(s, axis=-1, keepdims=True)
            e = jnp.exp(s - m)
            den = jnp.sum(e, axis=-1, keepdims=True)
            o = jnp.dot(e.astype(BF16), v[:, sl], preferred_element_type=F32)
            outs.append(o / den)
        o_ref[0, rows, :] = jnp.concatenate(outs, axis=-1).astype(BF16)


def _na_plan(rows):
    kr, kc = NA_ROWS, NA_COLS
    nblk = rows // NA_QROWS
    kr0s, case_ids, cases = [], [], {}
    tables = []
    qi = np.arange(NA_QROWS * GRID_W)
    ki = np.arange(NA_KROWS * GRID_W)
    qr_l, qc = qi // GRID_W, qi % GRID_W
    kr_l, kcol = ki // GRID_W, ki % GRID_W
    cstart = np.clip(qc - kc // 2, 0, GRID_W - kc)
    for b in range(nblk):
        r0 = b * NA_QROWS
        kr0 = int(np.clip(r0 - kr // 2, 0, rows - NA_KROWS))
        rstart = np.clip(r0 + np.arange(NA_QROWS) - kr // 2, 0, rows - kr)
        key = (r0 - kr0, tuple(int(t) for t in rstart - kr0))
        if key not in cases:
            cases[key] = len(cases)
            q_row = r0 + qr_l
            k_row = kr0 + kr_l
            rs = rstart[qr_l]
            in_row = (k_row[None, :] >= rs[:, None]) & (k_row[None, :] < rs[:, None] + kr)
            in_col = (kcol[None, :] >= cstart[:, None]) & (kcol[None, :] < cstart[:, None] + kc)
            ir = np.clip(k_row[None, :] - q_row[:, None] + NA_ROWS - 1, 0, 2 * NA_ROWS - 2)
            ic = np.clip(kcol[None, :] - qc[:, None] + NA_COLS - 1, 0, 2 * NA_COLS - 2)
            tables.append((in_row & in_col, ir, ic))
        kr0s.append(kr0)
        case_ids.append(cases[key])
    mask = np.stack([t[0] for t in tables])
    ir = np.stack([t[1] for t in tables])
    ic = np.stack([t[2] for t in tables])
    return np.asarray(kr0s, np.int32), np.asarray(case_ids, np.int32), mask, ir, ic


def _na_bias(rpb, mask, ir, ic):
    n_layers, n_heads = rpb.shape[:2]
    n_case = ir.shape[0]
    ir_s = ir[:, ::GRID_W, ::GRID_W].reshape(-1)
    ic_s = ic[0, :GRID_W, :GRID_W].reshape(-1)
    oh_r = jnp.asarray(np.eye(2 * NA_ROWS - 1, dtype=np.float32)[ir_s])
    oh_c = jnp.asarray(np.eye(2 * NA_COLS - 1, dtype=np.float32)[ic_s].T)
    b = jnp.einsum("xa,lhab,bp->lhxp", oh_r, rpb.astype(F32), oh_c,
                   precision=lax.Precision.HIGHEST)
    b = b.reshape(n_layers, n_heads, n_case, NA_QROWS, NA_KROWS, GRID_W, GRID_W)
    b = b.transpose(0, 2, 1, 3, 5, 4, 6).reshape(
        n_layers, n_case, n_heads, NA_QROWS * GRID_W, NA_KROWS * GRID_W)
    return jnp.where(mask[None, :, None], b, NEG_INF)


def _na_attention(main3, bias, kr0s, case_ids):
    n_seq, seq_len, _ = main3.shape
    tq = NA_QROWS * GRID_W * NA_SUB_BLOCKS
    grid = (n_seq, seq_len // tq)
    return pl.pallas_call(
        _na_kernel,
        grid_spec=pltpu.PrefetchScalarGridSpec(
            num_scalar_prefetch=2,
            grid=grid,
            in_specs=[
                pl.BlockSpec((1, tq, CHUNK), lambda s, i, kr, cs: (s, i, 0)),
                pl.BlockSpec((1, seq_len, CHUNK), lambda s, i, kr, cs: (s, 0, 1)),
                pl.BlockSpec((1, seq_len, CHUNK), lambda s, i, kr, cs: (s, 0, 2)),
                pl.BlockSpec(bias.shape, lambda s, i, kr, cs: (0, 0, 0, 0),
                             pipeline_mode=pl.Buffered(1)),
            ],
            out_specs=pl.BlockSpec((1, tq, CHUNK), lambda s, i, kr, cs: (s, i, 0)),
        ),
        out_shape=jax.ShapeDtypeStruct((n_seq, seq_len, CHUNK), BF16),
        compiler_params=_cparams(2),
        name="mixer_a",
    )(jnp.asarray(kr0s), jnp.asarray(case_ids), main3, main3, main3, bias)


def _band_kernel(*refs, window, n_kv, rep, length, has_sink, want_lse):
    refs = list(refs)
    sink_ref = refs.pop(0) if has_sink else None
    q_ref, k_ref, v_ref = refs[:3]
    o_ref = refs[3]
    lse_ref = refs[4] if want_lse else None
    tq = 2 * window
    kw = tq + 2 * window
    i = pl.program_id(1)
    for sb in range(q_ref.shape[1] // tq):
        rows = slice(sb * tq, (sb + 1) * tq)
        q0 = i * q_ref.shape[1] + sb * tq
        ks = pl.multiple_of(jnp.clip(q0 - window, 0, length - kw), 64)
        q = q_ref[0, rows, :]
        k = k_ref[0, pl.ds(ks, kw), :]
        v = v_ref[0, pl.ds(ks, kw), :]
        qpos = q0 + lax.broadcasted_iota(I32, (tq, kw), 0)
        kpos = ks + lax.broadcasted_iota(I32, (tq, kw), 1)
        valid = jnp.abs(qpos - kpos) <= window
        outs, lses = [], []
        for h in range(n_kv * rep):
            g = h // rep
            qs = slice(h * HEAD_DIM, (h + 1) * HEAD_DIM)
            kvs = slice(g * HEAD_DIM, (g + 1) * HEAD_DIM)
            s = lax.dot_general(q[:, qs], k[:, kvs], (((1,), (1,)), ((), ())),
                                preferred_element_type=F32)
            s = jnp.where(valid, s, NEG_INF)
            m = jnp.max(s, axis=-1, keepdims=True)
            if has_sink:
                m = jnp.maximum(m, sink_ref[h])
            e = jnp.exp(s - m)
            den = jnp.sum(e, axis=-1, keepdims=True)
            if has_sink:
                den = den + jnp.exp(sink_ref[h] - m)
            o = jnp.dot(e.astype(BF16), v[:, kvs], preferred_element_type=F32)
            outs.append(o / den)
            if want_lse:
                lses.append(jnp.broadcast_to(m + jnp.log(den), (tq, HEAD_DIM)))
        o_ref[0, rows, :] = jnp.concatenate(outs, axis=-1).astype(BF16)
        if want_lse:
            lse_ref[0, rows, :] = jnp.concatenate(lses, axis=-1)


def _band_attention(src3, q_blk, k_blk, v_blk, kv_width, window, n_kv, rep, sink=None,
                    want_lse=False):
    n_seq, length, _ = src3.shape
    tq = min(BAND_STEP_ROWS, length)
    qw = n_kv * rep * HEAD_DIM
    kern = functools.partial(_band_kernel, window=window, n_kv=n_kv, rep=rep, length=length,
                             has_sink=sink is not None, want_lse=want_lse)
    in_specs = [
        pl.BlockSpec((1, tq, qw), lambda s, i: (s, i, q_blk)),
        pl.BlockSpec((1, length, kv_width), lambda s, i: (s, 0, k_blk)),
        pl.BlockSpec((1, length, kv_width), lambda s, i: (s, 0, v_blk)),
    ]
    args = [src3, src3, src3]
    if sink is not None:
        in_specs = [pl.BlockSpec(memory_space=pltpu.SMEM)] + in_specs
        args = [sink.astype(F32)] + args
    o_spec = pl.BlockSpec((1, tq, qw), lambda s, i: (s, i, 0))
    out_specs = [o_spec]
    out_shape = [jax.ShapeDtypeStruct((n_seq, length, qw), BF16)]
    if want_lse:
        out_specs.append(o_spec)
        out_shape.append(jax.ShapeDtypeStruct((n_seq, length, qw), F32))
    res = pl.pallas_call(
        kern,
        grid=(n_seq, length // tq),
        in_specs=in_specs,
        out_specs=out_specs,
        out_shape=out_shape,
        compiler_params=_cparams(2),
        name="mixer_band",
    )(*args)
    return res


def _diff_kernel(lam_ref, q_ref, kt_ref, vx_ref, g_ref, o_ref, q_scr, m_scr, acc_scr, *, seq_len):
    tk = min(DIFF_TK, seq_len)
    n_sub = 2 * DIFF_HEADS
    q = q_ref[0]
    for j in range(n_sub):
        q_scr[j] = q[:, j * DIFF_QK_DIM:(j + 1) * DIFF_QK_DIM]
    acc_scr[...] = jnp.zeros(acc_scr.shape, F32)
    lam = lam_ref[0]
    post = lam_ref[1]
    bound = lam_ref[2]

    def operands(c, j):
        c0 = pl.multiple_of(c * tk, tk)
        kt = kt_ref[0, j * DIFF_QK_DIM:(j + 1) * DIFF_QK_DIM, pl.ds(c0, tk)]
        vv = vx_ref[0, pl.ds(c0, tk), (j // 2) * V7X_LANES:(j // 2 + 1) * V7X_LANES]
        return kt, vv

    @pl.when(bound <= DIFF_FAST_BOUND)
    def _():
        def body(c, carry):
            for j in range(n_sub):
                kt, vv = operands(c, j)
                s = jnp.dot(q_scr[j], kt, preferred_element_type=F32)
                p = jnp.exp2(s - bound).astype(BF16)
                acc_scr[j] += jnp.dot(p, vv, preferred_element_type=F32)
            return carry

        lax.fori_loop(0, seq_len // tk, body, 0)

    @pl.when(bound > DIFF_FAST_BOUND)
    def _():
        m_scr[...] = jnp.full(m_scr.shape, NEG_INF, F32)

        def body(c, carry):
            for j in range(n_sub):
                kt, vv = operands(c, j)
                s = jnp.dot(q_scr[j], kt, preferred_element_type=F32)
                mi = m_scr[j]
                mn = jnp.maximum(mi, jnp.max(s, axis=-1, keepdims=True))
                p = jnp.exp2(s - mn)
                alpha = jnp.exp2(mi - mn)
                acc_scr[j] = alpha * acc_scr[j] + jnp.dot(p.astype(BF16), vv,
                                                          preferred_element_type=F32)
                m_scr[j] = mn
            return carry

        lax.fori_loop(0, seq_len // tk, body, 0)

    outs = []
    for h in range(DIFF_HEADS):
        a0 = acc_scr[2 * h]
        a1 = acc_scr[2 * h + 1]
        od = a0[:, :HEAD_DIM] / a0[:, HEAD_DIM:] - lam * (a1[:, :HEAD_DIM] / a1[:, HEAD_DIM:])
        ms = jnp.mean(od * od, axis=-1, keepdims=True)
        outs.append(od * lax.rsqrt(ms + NORM_EPS) * g_ref[...] * post)
    o_ref[0] = jnp.concatenate(outs, axis=-1).astype(BF16)


def _diff_attention(main3, kdt, vdx3, lam_post, subln):
    n_seq, seq_len, _ = main3.shape
    tq = DIFF_TQ
    kern = functools.partial(_diff_kernel, seq_len=seq_len)
    return pl.pallas_call(
        kern,
        grid=(n_seq, seq_len // tq),
        in_specs=[
            pl.BlockSpec(memory_space=pltpu.SMEM),
            pl.BlockSpec((1, tq, CHUNK), lambda s, i: (s, i, MAIN_COLS // CHUNK - 1)),
            pl.BlockSpec((1, CHUNK, seq_len), lambda s, i: (s, 0, 0),
                         pipeline_mode=pl.Buffered(1)),
            pl.BlockSpec((1, seq_len, 2 * CHUNK), lambda s, i: (s, 0, 0),
                         pipeline_mode=pl.Buffered(1)),
            pl.BlockSpec((1, HEAD_DIM), lambda s, i: (0, 0)),
        ],
        out_specs=pl.BlockSpec((1, tq, CHUNK), lambda s, i: (s, i, 0)),
        out_shape=jax.ShapeDtypeStruct((n_seq, seq_len, CHUNK), BF16),
        scratch_shapes=[
            pltpu.VMEM((2 * DIFF_HEADS, tq, DIFF_QK_DIM), BF16),
            pltpu.VMEM((2 * DIFF_HEADS, tq, 1), F32),
            pltpu.VMEM((2 * DIFF_HEADS, tq, V7X_LANES), F32),
        ],
        compiler_params=_cparams(2),
        name="mixer_d",
    )(lam_post, main3, kdt, vdx3, subln)


def _out_proj_kernel(h_ref, oa_ref, ob_ref, oc0_ref, oc1_ref, oc2_ref, l0_ref, l1_ref, l2_ref,
                     od_ref, w_ref, fn_ref, wr_ref, h1_ref, hn_ref, aff_ref, *scratch):
    tm = h_ref.shape[0]

    def token_major(ref, scr):
        dil = ref.shape[1]
        halves = []
        for half_i in range(CHUNK // V7X_LANES):
            lanes = slice(half_i * V7X_LANES, (half_i + 1) * V7X_LANES)
            for r in range(dil):
                scr.at[half_i][pl.ds(r, tm // dil, stride=dil), :] = ref[0, r, :, lanes].astype(F32)
            halves.append(scr[half_i])
        return jnp.concatenate(halves, axis=-1)

    l0 = l0_ref[...]
    l1 = token_major(l1_ref, scratch[0])
    l2 = token_major(l2_ref, scratch[1])
    mx = jnp.maximum(jnp.maximum(l0, l1), l2)
    e0, e1, e2 = jnp.exp(l0 - mx), jnp.exp(l1 - mx), jnp.exp(l2 - mx)
    den = e0 + e1 + e2
    oc = ((e0 / den) * oc0_ref[...].astype(F32) + (e1 / den) * token_major(oc1_ref, scratch[2])
          + (e2 / den) * token_major(oc2_ref, scratch[3]))
    acc = h_ref[...]
    acc = acc + jnp.dot(oa_ref[...], w_ref[0 * CHUNK:1 * CHUNK, :], preferred_element_type=F32)
    acc = acc + jnp.dot(ob_ref[...], w_ref[1 * CHUNK:2 * CHUNK, :], preferred_element_type=F32)
    acc = acc + jnp.dot(oc.astype(BF16), w_ref[2 * CHUNK:3 * CHUNK, :], preferred_element_type=F32)
    acc = acc + jnp.dot(od_ref[...], w_ref[3 * CHUNK:4 * CHUNK, :], preferred_element_type=F32)
    h1_ref[...] = acc
    ms = jnp.mean(acc * acc, axis=-1, keepdims=True)
    hn = (acc * lax.rsqrt(ms + NORM_EPS) * fn_ref[...]).astype(BF16)
    hn_ref[...] = hn
    logits = jnp.dot(hn, wr_ref[...], preferred_element_type=F32)
    m = jnp.max(logits, axis=-1, keepdims=True)
    e = jnp.exp(logits - m)
    aff_ref[...] = e / jnp.sum(e, axis=-1, keepdims=True)


def _out_proj(h2, oa, ob, ocs, lses, od, w_out, ffn_norm, w_router, tm, seq_len):
    n = h2.shape[0]
    nblk = seq_len // tm
    row = lambda w: pl.BlockSpec((tm, w), lambda i: (i, 0))
    const = lambda r, c: pl.BlockSpec((r, c), lambda i: (0, 0))
    dil = lambda g: _dilated_spec(DIL_CONFIGS[g][1], tm, CHUNK, nblk)
    return pl.pallas_call(
        _out_proj_kernel,
        grid=(n // tm,),
        in_specs=[row(D_MODEL), row(CHUNK), row(CHUNK), row(CHUNK), dil(1), dil(2),
                  row(CHUNK), dil(1), dil(2), row(CHUNK),
                  const(D_MODEL, D_MODEL), const(1, D_MODEL), const(D_MODEL, N_EXPERTS)],
        out_specs=[row(D_MODEL), row(D_MODEL), row(N_EXPERTS)],
        out_shape=[
            jax.ShapeDtypeStruct((n, D_MODEL), F32),
            jax.ShapeDtypeStruct((n, D_MODEL), BF16),
            jax.ShapeDtypeStruct((n, N_EXPERTS), F32),
        ],
        scratch_shapes=[pltpu.VMEM((CHUNK // V7X_LANES, tm, V7X_LANES), F32)] * 4,
        compiler_params=_cparams(1),
        name="out_proj",
    )(h2, oa, ob, ocs[0], ocs[1], ocs[2], lses[0], lses[1], lses[2], od, w_out, ffn_norm,
      w_router)


def _select_kernel(aff_ref, posm_ref, pref_ref, *, cap, n_tok):
    def bits():
        return lax.bitcast_convert_type(aff_ref[...], I32)

    def bit_body(i, thr):
        cand = thr | jnp.left_shift(jnp.int32(1), 30 - i)
        cnt = jnp.sum(jnp.where(bits() >= cand, 1.0, 0.0), axis=-1, keepdims=True)
        return jnp.where(cnt >= cap, cand, thr)

    thr = lax.fori_loop(0, 31, bit_body, jnp.zeros((N_EXPERTS, 1), I32))
    cnt_gt = jnp.sum(jnp.where(bits() > thr, 1.0, 0.0), axis=-1, keepdims=True)
    need = cap - cnt_gt
    blk = SLOT_BLK
    r = lax.broadcasted_iota(I32, (blk, blk), 0)
    c = lax.broadcasted_iota(I32, (blk, blk), 1)
    upper = jnp.where(r < c, 1.0, 0.0).astype(BF16)

    def blk_body(b, carry):
        ceq, csel = carry
        b0 = pl.multiple_of(b * blk, blk)
        xb = lax.bitcast_convert_type(aff_ref[:, pl.ds(b0, blk)], I32)
        gt = xb > thr
        eq = xb == thr
        eqf = jnp.where(eq, 1.0, 0.0)
        rank = ceq + jnp.dot(eqf.astype(BF16), upper, preferred_element_type=F32)
        sel = gt | (eq & (rank < need))
        self_ = jnp.where(sel, 1.0, 0.0)
        pref = csel + jnp.dot(self_.astype(BF16), upper, preferred_element_type=F32)
        prefi = pref.astype(I32)
        pref_ref[:, pl.ds(b0, blk)] = prefi
        posm_ref[:, pl.ds(b0, blk)] = jnp.where(sel, prefi, -1)
        return (ceq + jnp.sum(eqf, axis=-1, keepdims=True),
                csel + jnp.sum(self_, axis=-1, keepdims=True))

    zero = jnp.zeros((N_EXPERTS, 1), F32)
    lax.fori_loop(0, n_tok // blk, blk_body, (zero, zero))


def _select(aff_t, cap):
    n_tok = aff_t.shape[1]
    kern = functools.partial(_select_kernel, cap=cap, n_tok=n_tok)
    full = pl.BlockSpec((N_EXPERTS, n_tok), lambda i: (0, 0))
    return pl.pallas_call(
        kern,
        grid=(1,),
        in_specs=[full],
        out_specs=[full, full],
        out_shape=[jax.ShapeDtypeStruct((N_EXPERTS, n_tok), I32)] * 2,
        compiler_params=_cparams(1),
        name="select",
    )(aff_t)


_VALID, _FIRST, _LAST = 1, 2, 4


_NO_HIT = 1 << 30


def _block_windows(pref, cap, tok_blk, slot_blk):
    n_exp = pref.shape[0]
    nw = cap // slot_blk
    off = pref[:, ::tok_blk]
    end = jnp.concatenate([off[:, 1:], jnp.full((n_exp, 1), cap, I32)], axis=1)
    cnt = end - off
    w_lo = jnp.minimum(off // slot_blk, nw - 1)
    w_hi = jnp.where(cnt > 0, (end - 1) // slot_blk, w_lo)
    return cnt, w_lo, w_hi


def _search(starts, idx):
    return (jnp.sum(starts[None, :] <= idx.reshape(-1, 1), axis=1) - 1).astype(I32).reshape(idx.shape)


def _gather_plan(pref, cap):
    n_exp, n_tok = pref.shape
    nb = n_tok // GATHER_TOK_BLK
    nw = cap // SLOT_BLK
    _, w_lo, w_hi = _block_windows(pref, cap, GATHER_TOK_BLK, SLOT_BLK)
    s_g = nb + nw
    nst = w_hi - w_lo + 1
    start = jnp.cumsum(nst, axis=1) - nst
    total = jnp.sum(nst, axis=1, keepdims=True)
    s_idx = jnp.arange(s_g, dtype=I32)[None, :]
    s_eff = jnp.minimum(s_idx, total - 1)
    b_s = jax.vmap(_search)(start, s_eff)
    w_s = jnp.take_along_axis(w_lo, b_s, axis=1) + (s_eff - jnp.take_along_axis(start, b_s, axis=1))
    valid = s_idx < total
    prev_w = jnp.concatenate([jnp.full((n_exp, 1), -1, I32), w_s[:, :-1]], axis=1)
    next_w = jnp.concatenate([w_s[:, 1:], jnp.full((n_exp, 1), -1, I32)], axis=1)
    first = valid & (w_s != prev_w)
    last = valid & ((w_s != next_w) | (s_idx == total - 1))
    flag = valid * _VALID + first * _FIRST + last * _LAST
    return (b_s.reshape(-1), w_s.reshape(-1).astype(I32), flag.reshape(-1).astype(I32)), s_g


def _combine_plan(pref, cap):
    n_exp, n_tok = pref.shape
    nb = n_tok // COMBINE_TOK_BLK
    nw = cap // COMBINE_SLOT_BLK
    grp = COMBINE_GROUP
    cnt, w_lo, w_hi = _block_windows(pref, cap, COMBINE_TOK_BLK, COMBINE_SLOT_BLK)
    cnt_t, wlo_t, whi_t = cnt.T, w_lo.T, w_hi.T
    n_win = jnp.where(cnt_t > 0, whi_t - wlo_t + 1, 0)
    win_start = (jnp.cumsum(n_win.reshape(-1)) - n_win.reshape(-1)).astype(I32)
    per_blk = jnp.sum(n_win, axis=1)
    blk_first_win = (jnp.cumsum(per_blk) - per_blk).astype(I32)
    steps = jnp.maximum((per_blk + grp - 1) // grp, 1)
    step_start = (jnp.cumsum(steps) - steps).astype(I32)
    total = jnp.sum(steps)
    s_c = (n_exp * (nb + nw) + grp - 1) // grp + nb
    s_idx = jnp.arange(s_c, dtype=I32)
    s_eff = jnp.minimum(s_idx, total - 1)
    cb = _search(step_start, s_eff)
    local = s_eff - step_start[cb]
    q = local[:, None] * grp + jnp.arange(grp, dtype=I32)[None, :]
    live = (q < per_blk[cb][:, None]) & (s_idx < total)[:, None]
    win = jnp.where(live, blk_first_win[cb][:, None] + q, 0)
    pair = _search(win_start, win.reshape(-1)).reshape(win.shape)
    ce = pair % n_exp
    cw = wlo_t.reshape(-1)[pair] + (win - win_start[pair])
    cw = jnp.clip(cw, 0, nw - 1)
    blk = (ce * nw + cw).astype(I32)
    base = jnp.where(live, cw * COMBINE_SLOT_BLK, _NO_HIT).astype(I32)
    valid = s_idx < total
    prev_b = jnp.concatenate([jnp.full((1,), -1, I32), cb[:-1]])
    next_b = jnp.concatenate([cb[1:], jnp.full((1,), -1, I32)])
    first = valid & (cb != prev_b)
    last = valid & ((cb != next_b) | (s_idx == total - 1))
    flag = (valid * _VALID + first * _FIRST + last * _LAST).astype(I32)
    return (cb, ce.reshape(-1).astype(I32), blk.reshape(-1), base.reshape(-1), flag), s_c


def _gather_kernel(gb_ref, gw_ref, gf_ref, posm_ref, g_ref, hn_ref, xe_ref, gs_ref, acc_ref,
                   gacc_ref, *, n_steps):
    e = pl.program_id(0)
    s = pl.program_id(1)
    flag = gf_ref[e * n_steps + s]
    w = gw_ref[e * n_steps + s]

    @pl.when((flag & _FIRST) != 0)
    def _():
        acc_ref[...] = jnp.zeros_like(acc_ref)
        gacc_ref[...] = jnp.zeros_like(gacc_ref)

    @pl.when((flag & _VALID) != 0)
    def _():
        rel = posm_ref[0] - w * SLOT_BLK
        hit = lax.broadcasted_iota(I32, (SLOT_BLK, GATHER_TOK_BLK), 0) == rel
        onehot = jnp.where(hit, 1.0, 0.0).astype(BF16)
        acc_ref[...] += jnp.dot(onehot, hn_ref[...], preferred_element_type=F32)
        gacc_ref[...] += jnp.sum(jnp.where(hit, g_ref[0], 0.0), axis=-1, keepdims=True)

    @pl.when((flag & _LAST) != 0)
    def _():
        xe_ref[0] = acc_ref[...].astype(BF16)
        gs_ref[0] = gacc_ref[...]


def _gather(plan, s_g, posm3, g3, hn, cap):
    gb, gw, gf = plan
    n_tok = hn.shape[0]
    kern = functools.partial(_gather_kernel, n_steps=s_g)
    tok = lambda e, s, gb, gw, gf: (e, 0, gb[e * s_g + s])
    slot = lambda e, s, gb, gw, gf: (e, gw[e * s_g + s], 0)
    return pl.pallas_call(
        kern,
        grid_spec=pltpu.PrefetchScalarGridSpec(
            num_scalar_prefetch=3,
            grid=(N_EXPERTS, s_g),
            in_specs=[
                pl.BlockSpec((1, 1, GATHER_TOK_BLK), tok),
                pl.BlockSpec((1, 1, GATHER_TOK_BLK), tok),
                pl.BlockSpec((GATHER_TOK_BLK, D_MODEL), lambda e, s, gb, gw, gf: (gb[e * s_g + s], 0)),
            ],
            out_specs=[
                pl.BlockSpec((1, SLOT_BLK, D_MODEL), slot),
                pl.BlockSpec((1, SLOT_BLK, 1), slot),
            ],
            scratch_shapes=[pltpu.VMEM((SLOT_BLK, D_MODEL), F32), pltpu.VMEM((SLOT_BLK, 1), F32)],
        ),
        out_shape=[
            jax.ShapeDtypeStruct((N_EXPERTS, cap, D_MODEL), BF16),
            jax.ShapeDtypeStruct((N_EXPERTS, cap, 1), F32),
        ],
        compiler_params=_cparams(2),
        name="gather",
    )(gb, gw, gf, posm3, g3, hn)


def _experts_kernel(x_ref, gs_ref, wg_ref, wu_ref, wd_ref, y_ref):
    x = x_ref[0]
    hg = jnp.dot(x, wg_ref[0], preferred_element_type=F32)
    hu = jnp.dot(x, wu_ref[0], preferred_element_type=F32)
    hdn = (hg / (1.0 + jnp.exp(-hg)) * hu).astype(BF16)
    y = jnp.dot(hdn, wd_ref[0], preferred_element_type=F32)
    y_ref[0] = (y * gs_ref[0]).astype(BF16)


def _experts(xe, gs, w_gate, w_up, w_down):
    n_exp, cap, _ = xe.shape
    tc = min(512, cap)
    wspec = pl.BlockSpec((1, D_MODEL, D_MODEL), lambda e, c: (e, 0, 0))
    return pl.pallas_call(
        _experts_kernel,
        grid=(n_exp, cap // tc),
        in_specs=[
            pl.BlockSpec((1, tc, D_MODEL), lambda e, c: (e, c, 0)),
            pl.BlockSpec((1, tc, 1), lambda e, c: (e, c, 0)),
            wspec, wspec, wspec,
        ],
        out_specs=pl.BlockSpec((1, tc, D_MODEL), lambda e, c: (e, c, 0)),
        out_shape=jax.ShapeDtypeStruct((n_exp, cap, D_MODEL), BF16),
        compiler_params=_cparams(2),
        name="experts",
    )(xe, gs, w_gate, w_up, w_down)


def _combine_kernel(cb_ref, ce_ref, cblk_ref, cbase_ref, cf_ref, h1_ref, posmt_ref, *rest):
    del cb_ref, cblk_ref
    grp = COMBINE_GROUP
    ye_refs = rest[:grp]
    p_ref, pn_ref, wpg_ref, wpp_ref, o_ref, acc_ref = rest[grp:]
    s = pl.program_id(0)
    flag = cf_ref[s]

    @pl.when((flag & _FIRST) != 0)
    def _():
        acc_ref[...] = jnp.zeros_like(acc_ref)

    @pl.when((flag & _VALID) != 0)
    def _():
        pm = posmt_ref[...].astype(F32)
        lane = lax.broadcasted_iota(I32, pm.shape, 1)
        slot_iota = lax.broadcasted_iota(I32, (COMBINE_TOK_BLK, COMBINE_SLOT_BLK), 1)
        total = None
        for k in range(grp):
            e = ce_ref[s * grp + k]
            col = jnp.sum(jnp.where(lane == e, pm, 0.0), axis=-1, keepdims=True)
            rel = col.astype(I32) - cbase_ref[s * grp + k]
            onehot = jnp.where(slot_iota == rel, 1.0, 0.0).astype(BF16)
            part = jnp.dot(onehot, ye_refs[k][0], preferred_element_type=F32)
            total = part if total is None else total + part
        acc_ref[...] += total

    @pl.when((flag & _LAST) != 0)
    def _():
        h2 = h1_ref[...] + acc_ref[...]
        ms = jnp.mean(h2 * h2, axis=-1, keepdims=True)
        hn = (h2 * lax.rsqrt(ms + NORM_EPS) * pn_ref[...]).astype(BF16)
        z = jnp.dot(hn, wpg_ref[...], preferred_element_type=F32)
        gate = 1.0 / (1.0 + jnp.exp(-z))
        proj = jnp.dot(p_ref[...].astype(BF16), wpp_ref[...], preferred_element_type=F32)
        o_ref[...] = h2 + gate * proj


def _combine(plan, s_c, h1, posm_t, ye, p2, ple_norm, w_ple_gate, w_ple_proj, cap):
    cb, ce, cblk, cbase, cf = plan
    n_tok = h1.shape[0]
    nw = cap // COMBINE_SLOT_BLK
    grp = COMBINE_GROUP
    ye3 = ye.reshape(N_EXPERTS * nw, COMBINE_SLOT_BLK, D_MODEL)
    tokb = lambda width: pl.BlockSpec((COMBINE_TOK_BLK, width), lambda s, cb, *_: (cb[s], 0))
    const = lambda r, c: pl.BlockSpec((r, c), lambda s, *_: (0, 0))

    def ye_spec(k):
        return pl.BlockSpec((1, COMBINE_SLOT_BLK, D_MODEL),
                            lambda s, cb, ce, cblk, *_: (cblk[s * grp + k], 0, 0))

    return pl.pallas_call(
        _combine_kernel,
        grid_spec=pltpu.PrefetchScalarGridSpec(
            num_scalar_prefetch=5,
            grid=(s_c,),
            in_specs=[tokb(D_MODEL), tokb(N_EXPERTS)] + [ye_spec(k) for k in range(grp)] + [
                tokb(PLE_DIM), const(1, D_MODEL), const(D_MODEL, D_MODEL),
                const(PLE_DIM, D_MODEL)],
            out_specs=tokb(D_MODEL),
            scratch_shapes=[pltpu.VMEM((COMBINE_TOK_BLK, D_MODEL), F32)],
        ),
        out_shape=jax.ShapeDtypeStruct((n_tok, D_MODEL), F32),
        compiler_params=_cparams(1),
        name="combine",
    )(cb, ce, cblk, cbase, cf, h1, posm_t, *([ye3] * grp), p2, ple_norm, w_ple_gate, w_ple_proj)


def _layer(h2, p2, lw, static):
    n_seq, seq_len, tabs, b64, b32, na_plan = static
    n_tok = h2.shape[0]
    tm = min(512, seq_len)
    gains = _in_gains(lw["qk_norm_a"], lw["qk_norm_b"], lw["qk_norm_c"], lw["qk_norm_d"])
    main, cg1, cg2, kdt, vdx = _in_proj(h2, lw["attn_norm"][None, :], lw["w_in"], b64, b32, gains,
                                        tabs, n_seq, seq_len, tm)
    main3 = main.reshape(n_seq, seq_len, MAIN_COLS)

    kr0s, case_ids = na_plan
    oa = _na_attention(main3, lw["na_bias"], kr0s, case_ids)

    (ob,) = _band_attention(main3, 3, 8, 9, V7X_LANES, SW_WINDOW, SW_KV_HEADS,
                            SW_HEADS // SW_KV_HEADS, sink=lw["sink_b"])
    ocs, lses = [], []
    for g_i, (win, dil) in enumerate(DIL_CONFIGS):
        if dil == 1:
            src, cols = main3, (5, 6, 7)
        else:
            src = (cg1, cg2)[g_i - 1].reshape(n_seq * dil, seq_len // dil, 3 * CHUNK)
            cols = (0, 1, 2)
        o_g, l_g = _band_attention(src, *cols, CHUNK, win // (2 * dil), DIL_HEADS, 1,
                                   want_lse=True)
        shape = (n_tok, CHUNK) if dil == 1 else (n_seq, dil, seq_len // dil, CHUNK)
        ocs.append(o_g.reshape(shape))
        lses.append(l_g.reshape(shape))

    lf = lw["lambda_d"].astype(F32)
    lam = jnp.exp(jnp.sum(lf[0] * lf[1])) - jnp.exp(jnp.sum(lf[2] * lf[3])) + lw["lam_init"]
    gd = jnp.abs(lw["qk_norm_d"].astype(F32))
    score_bound = (LOG2_E * DIFF_QK_DIM ** 0.5 * (1.0 + 2.0 ** -6)) * jnp.max(gd[0]) * jnp.max(gd[1])
    lam_post = jnp.stack([lam, 1.0 - lw["lam_init"], score_bound]).astype(F32)
    od = _diff_attention(main3, kdt, vdx.reshape(n_seq, seq_len, 2 * CHUNK), lam_post,
                         lw["subln_d"].astype(F32)[None, :])

    h1, hn, aff = _out_proj(h2, oa.reshape(n_tok, CHUNK), ob.reshape(n_tok, CHUNK), ocs, lses,
                            od.reshape(n_tok, CHUNK), lw["w_out"], lw["ffn_norm"][None, :],
                            lw["w_router"], tm, seq_len)

    cap = EC_CAPACITY_FACTOR * n_tok // N_EXPERTS
    aff_t = aff.T
    posm, pref = _select(aff_t, cap)
    gather_plan, s_g = _gather_plan(pref, cap)
    combine_plan, s_c = _combine_plan(pref, cap)
    xe, gs = _gather(gather_plan, s_g, posm.reshape(N_EXPERTS, 1, n_tok),
                     aff_t.reshape(N_EXPERTS, 1, n_tok), hn, cap)
    ye = _experts(xe, gs, lw["w_gate"], lw["w_up"], lw["w_down"])
    return _combine(combine_plan, s_c, h1, posm.T, ye, p2, lw["ple_norm"][None, :],
                    lw["w_ple_gate"], lw["w_ple_proj"], cap)


def _trunk(x, p, weights):
    n_seq, seq_len, _ = x.shape
    depth = p.shape[0]
    tabs = _rope_tables(seq_len, HEAD_DIM // 2, 4) + _rope_tables(seq_len, DIFF_QK_DIM // 2, 8)
    kr0s, case_ids, mask, ir, ic = _na_plan(seq_len // GRID_W)
    static = (n_seq, seq_len, tabs, _block_mean_matrix(HEAD_DIM), _block_mean_matrix(DIFF_QK_DIM),
              (kr0s, case_ids))
    xs = dict(weights)
    xs["p"] = p.reshape(depth, n_seq * seq_len, PLE_DIM)
    xs["na_bias"] = _na_bias(weights["rpb_a"], mask, ir, ic)

    def step(h2, lw):
        return _layer(h2, lw["p"], lw, static), None

    h2, _ = lax.scan(step, x.reshape(n_seq * seq_len, D_MODEL), xs)
    return h2.reshape(n_seq, seq_len, D_MODEL)


def kernel(x_prompt, x_sample, p_prompt, p_sample, attn_norm, w_in, rpb_a, qk_norm_a, qk_norm_b, sink_b, qk_norm_c, qk_norm_d, lambda_d, subln_d, w_out, ffn_norm, w_router, w_gate, w_up, w_down, ple_norm, w_ple_gate, w_ple_proj):
    depth = w_in.shape[0]
    lam_init = jnp.asarray([0.8 - 0.6 * math.exp(-0.3 * i) for i in range(depth)], F32)
    weights = dict(
        attn_norm=attn_norm.astype(F32), w_in=jax.vmap(_extend_w_in)(w_in), rpb_a=rpb_a,
        qk_norm_a=qk_norm_a, qk_norm_b=qk_norm_b, sink_b=sink_b, qk_norm_c=qk_norm_c,
        qk_norm_d=qk_norm_d, lambda_d=lambda_d, subln_d=subln_d, w_out=w_out.astype(BF16),
        ffn_norm=ffn_norm.astype(F32), w_router=w_router.astype(BF16), w_gate=w_gate.astype(BF16),
        w_up=w_up.astype(BF16), w_down=w_down.astype(BF16), ple_norm=ple_norm.astype(F32),
        w_ple_gate=w_ple_gate.astype(BF16), w_ple_proj=w_ple_proj.astype(BF16), lam_init=lam_init)
    y_prompt = _trunk(x_prompt, p_prompt, weights)
    y_sample = _trunk(x_sample, p_sample, weights)
    return (y_prompt, y_sample)
```

```python
import functools
import math

import jax
import jax.numpy as jnp
import numpy as np
from jax import lax
from jax.experimental import pallas as pl
from jax.experimental.pallas import tpu as pltpu

F32 = jnp.float32
BF16 = jnp.bfloat16
I32 = jnp.int32

D_MODEL = 1024
HEAD_DIM = 64
GRID_W = 64
NA_HEADS = 4
NA_ROWS = 8
NA_COLS = 16
SW_HEADS = 4
SW_KV_HEADS = 2
SW_WINDOW = 128
DIL_CONFIGS = ((128, 1), (512, 4), (2048, 16))
DIL_HEADS = 4
DIFF_HEADS = 4
DIFF_QK_DIM = HEAD_DIM // 2
ROPE_THETA = 10000.0
N_EXPERTS = 16
EC_CAPACITY_FACTOR = 2
PLE_DIM = 256
NEG_INF = -1e30
NORM_EPS = 1e-6

V7X_LANES = 128
V7X_VMEM_LIMIT = 56 * 1024 * 1024

CHUNK = 256
MAIN_COLS = 9 * CHUNK
N_IN_CHUNKS = 18
NA_QROWS = 2
NA_KROWS = NA_QROWS + NA_ROWS - 1
NA_SUB_BLOCKS = 4
BAND_STEP_ROWS = 512
DIFF_TQ = 1024
DIFF_TK = 512
DIFF_FAST_BOUND = 40.0
LOG2_E = math.log2(math.e)
SLOT_BLK = 256
GATHER_TOK_BLK = 2048
COMBINE_TOK_BLK = 512
COMBINE_SLOT_BLK = 128
COMBINE_GROUP = 8


def _cparams(n_axes):
    return pltpu.CompilerParams(
        dimension_semantics=("arbitrary",) * n_axes, vmem_limit_bytes=V7X_VMEM_LIMIT)


_IN_CHUNK_KIND = (
    (64, 0), (64, 0), (0, 0),
    (64, 32), (64, 32),
    (64, 32), (64, 32), (64, 32),
    (64, 32), (64, 32), (64, 32),
    (0, 0), (0, 0), (0, 0),
    (32, 16),
    (32, 16),
    (0, 0), (0, 0),
)
_KBVB_CHUNK = 4
_IN_CHUNK_DEST = (
    ("main", 0), ("main", 1), ("main", 2), ("main", 3), ("main", 4),
    ("main", 5), ("dil", (0, 0)), ("dil", (1, 0)),
    ("main", 6), ("dil", (0, 1)), ("dil", (1, 1)),
    ("main", 7), ("dil", (0, 2)), ("dil", (1, 2)),
    ("main", 8), ("kdt", 0), ("vdx", 0), ("vdx", 1),
)


def _in_proj_kernel(h_ref, gn_ref, w_ref, b64_ref, b32_ref, gains_ref, cs64_ref, sn64_ref,
                    cs32_ref, sn32_ref, main_ref, cg1_ref, cg2_ref, kdt_ref, vdx_ref, row_scr):
    x = h_ref[...]
    ms = jnp.mean(x * x, axis=-1, keepdims=True)
    xn = (x * lax.rsqrt(ms + NORM_EPS) * gn_ref[...]).astype(BF16)
    tm = x.shape[0]
    lane = lax.broadcasted_iota(I32, (tm, CHUNK), 1)
    for c in range(N_IN_CHUNKS):
        hd, half = _IN_CHUNK_KIND[c]
        y = jnp.dot(xn, w_ref[:, c * CHUNK:(c + 1) * CHUNK], preferred_element_type=F32)
        out = y
        if hd:
            bmat = b64_ref[...] if hd == 64 else b32_ref[...]
            msq = jnp.dot((y * y).astype(BF16), bmat, preferred_element_type=F32)
            out = y * lax.rsqrt(msq + NORM_EPS) * gains_ref[c:c + 1, :]
        if half:
            cs = cs64_ref[...] if half == 32 else cs32_ref[...]
            sn = sn64_ref[...] if half == 32 else sn32_ref[...]
            fwd = pltpu.roll(out, half, 1)
            bwd = pltpu.roll(out, CHUNK - half, 1)
            first = (lane & (2 * half - 1)) < half
            out = out * cs + jnp.where(first, bwd, fwd) * sn
        if c == _KBVB_CHUNK:
            out = jnp.where(lane < CHUNK // 2, out, y)
        kind, where = _IN_CHUNK_DEST[c]
        if kind == "main":
            main_ref[:, where * CHUNK:(where + 1) * CHUNK] = out.astype(BF16)
        elif kind == "dil":
            group, part = where
            dst = (cg1_ref, cg2_ref)[group]
            dil = dst.shape[1]
            for half_i in range(CHUNK // V7X_LANES):
                lanes = slice(half_i * V7X_LANES, (half_i + 1) * V7X_LANES)
                row_scr[half_i] = out[:, lanes]
                for r in range(dil):
                    col0 = part * CHUNK + half_i * V7X_LANES
                    dst[0, r, :, col0:col0 + V7X_LANES] = row_scr.at[half_i][
                        pl.ds(r, tm // dil, stride=dil), :].astype(BF16)
        elif kind == "kdt":
            kdt_ref[0] = out.T.astype(BF16)
        else:
            ones = jnp.where((lane & (V7X_LANES - 1)) >= HEAD_DIM, 1.0, 0.0)
            vdx_ref[:, where * CHUNK:(where + 1) * CHUNK] = (out + ones).astype(BF16)


def _dilated_spec(dil, tm, width, nblk):
    return pl.BlockSpec((1, dil, tm // dil, width), lambda i: (i // nblk, 0, i % nblk, 0))


def _in_proj(h2, gn, w_ext, b64, b32, gains, tabs, n_seq, seq_len, tm):
    n = h2.shape[0]
    nblk = seq_len // tm
    d1, d2 = DIL_CONFIGS[1][1], DIL_CONFIGS[2][1]
    cs64, sn64, cs32, sn32 = tabs
    const = lambda i: (0, 0)
    tab = lambda i: (i % nblk, 0)
    return pl.pallas_call(
        _in_proj_kernel,
        grid=(n // tm,),
        in_specs=[
            pl.BlockSpec((tm, D_MODEL), lambda i: (i, 0)),
            pl.BlockSpec((1, D_MODEL), const),
            pl.BlockSpec((D_MODEL, N_IN_CHUNKS * CHUNK), const),
            pl.BlockSpec((CHUNK, CHUNK), const),
            pl.BlockSpec((CHUNK, CHUNK), const),
            pl.BlockSpec((N_IN_CHUNKS, CHUNK), const),
            pl.BlockSpec((tm, CHUNK), tab),
            pl.BlockSpec((tm, CHUNK), tab),
            pl.BlockSpec((tm, CHUNK), tab),
            pl.BlockSpec((tm, CHUNK), tab),
        ],
        out_specs=[
            pl.BlockSpec((tm, MAIN_COLS), lambda i: (i, 0)),
            _dilated_spec(d1, tm, 3 * CHUNK, nblk),
            _dilated_spec(d2, tm, 3 * CHUNK, nblk),
            pl.BlockSpec((1, CHUNK, tm), lambda i: (i // nblk, 0, i % nblk)),
            pl.BlockSpec((tm, 2 * CHUNK), lambda i: (i, 0)),
        ],
        out_shape=[
            jax.ShapeDtypeStruct((n, MAIN_COLS), BF16),
            jax.ShapeDtypeStruct((n_seq, d1, seq_len // d1, 3 * CHUNK), BF16),
            jax.ShapeDtypeStruct((n_seq, d2, seq_len // d2, 3 * CHUNK), BF16),
            jax.ShapeDtypeStruct((n_seq, CHUNK, seq_len), BF16),
            jax.ShapeDtypeStruct((n, 2 * CHUNK), BF16),
        ],
        scratch_shapes=[pltpu.VMEM((CHUNK // V7X_LANES, tm, V7X_LANES), F32)],
        compiler_params=_cparams(1),
        name="in_proj",
    )(h2, gn, w_ext, b64, b32, gains, cs64, sn64, cs32, sn32)


def _rope_tables(seq_len, half, reps):
    inv_freq = ROPE_THETA ** (-jnp.arange(half, dtype=F32) / half)
    ang = jnp.arange(seq_len, dtype=F32)[:, None] * inv_freq[None, :]
    cos, sin = jnp.cos(ang), jnp.sin(ang)
    cs = jnp.tile(jnp.concatenate([cos, cos], axis=-1), (1, reps))
    sn = jnp.tile(jnp.concatenate([-sin, sin], axis=-1), (1, reps))
    return cs, sn


def _block_mean_matrix(head):
    blocks = CHUNK // head
    return jnp.asarray(np.kron(np.eye(blocks), np.full((head, head), 1.0 / head)), BF16)


def _extend_w_in(w_in):
    vd = w_in[:, 4096:4352].reshape(D_MODEL, DIFF_HEADS, HEAD_DIM)
    vdx = jnp.concatenate([vd, jnp.zeros_like(vd)], axis=-1).reshape(D_MODEL, 2 * CHUNK)
    return jnp.concatenate([w_in[:, :4096], vdx], axis=-1).astype(BF16)


def _in_gains(qk_a, qk_b, qk_c, qk_d):
    one = jnp.ones((CHUNK,), F32)
    s64 = HEAD_DIM ** -0.5
    s32 = DIFF_QK_DIM ** -0.5
    t4 = lambda g: jnp.tile(g.astype(F32), 4)
    rows = [
        t4(qk_a[0]) * s64, t4(qk_a[1]), one,
        t4(qk_b[0]) * s64, jnp.concatenate([jnp.tile(qk_b[1].astype(F32), 2), jnp.ones((128,), F32)]),
        t4(qk_c[0]) * s64, t4(qk_c[0]) * s64, t4(qk_c[0]) * s64,
        t4(qk_c[1]), t4(qk_c[1]), t4(qk_c[1]),
        one, one, one,
        jnp.tile(qk_d[0].astype(F32), 8) * (s32 * LOG2_E),
        jnp.tile(qk_d[1].astype(F32), 8),
        one, one,
    ]
    return jnp.stack(rows)


def _na_kernel(kr0_ref, case_ref, q_ref, k_ref, v_ref, bias_ref, o_ref):
    i = pl.program_id(1)
    tq = NA_QROWS * GRID_W
    nk = NA_KROWS * GRID_W
    n_sub = q_ref.shape[1] // tq
    for sb in range(n_sub):
        blk = i * n_sub + sb
        rows = slice(sb * tq, (sb + 1) * tq)
        ks = pl.multiple_of(kr0_ref[blk] * GRID_W, GRID_W)
        case = case_ref[blk]
        q = q_ref[0, rows, :]
        k = k_ref[0, pl.ds(ks, nk), :]
        v = v_ref[0, pl.ds(ks, nk), :]
        outs = []
        for h in range(NA_HEADS):
            sl = slice(h * HEAD_DIM, (h + 1) * HEAD_DIM)
            s = lax.dot_general(q[:, sl], k[:, sl], (((1,), (1,)), ((), ())),
                                preferred_element_type=F32)
            s = s + bias_ref[case, h]
            m = jnp.max(s, axis=-1, keepdims=True)
            e = jnp.exp(s - m)
            den = jnp.sum(e, axis=-1, keepdims=True)
            o = jnp.dot(e.astype(BF16), v[:, sl], preferred_element_type=F32)
            outs.append(o / den)
        o_ref[0, rows, :] = jnp.concatenate(outs, axis=-1).astype(BF16)


def _na_plan(rows):
    kr, kc = NA_ROWS, NA_COLS
    nblk = rows // NA_QROWS
    kr0s, case_ids, cases = [], [], {}
    tables = []
    qi = np.arange(NA_QROWS * GRID_W)
    ki = np.arange(NA_KROWS * GRID_W)
    qr_l, qc = qi // GRID_W, qi % GRID_W
    kr_l, kcol = ki // GRID_W, ki % GRID_W
    cstart = np.clip(qc - kc // 2, 0, GRID_W - kc)
    for b in range(nblk):
        r0 = b * NA_QROWS
        kr0 = int(np.clip(r0 - kr // 2, 0, rows - NA_KROWS))
        rstart = np.clip(r0 + np.arange(NA_QROWS) - kr // 2, 0, rows - kr)
        key = (r0 - kr0, tuple(int(t) for t in rstart - kr0))
        if key not in cases:
            cases[key] = len(cases)
            q_row = r0 + qr_l
            k_row = kr0 + kr_l
            rs = rstart[qr_l]
            in_row = (k_row[None, :] >= rs[:, None]) & (k_row[None, :] < rs[:, None] + kr)
            in_col = (kcol[None, :] >= cstart[:, None]) & (kcol[None, :] < cstart[:, None] + kc)
            ir = np.clip(k_row[None, :] - q_row[:, None] + NA_ROWS - 1, 0, 2 * NA_ROWS - 2)
            ic = np.clip(kcol[None, :] - qc[:, None] + NA_COLS - 1, 0, 2 * NA_COLS - 2)
            tables.append((in_row & in_col, ir, ic))
        kr0s.append(kr0)
        case_ids.append(cases[key])
    mask = np.stack([t[0] for t in tables])
    ir = np.stack([t[1] for t in tables])
    ic = np.stack([t[2] for t in tables])
    return np.asarray(kr0s, np.int32), np.asarray(case_ids, np.int32), mask, ir, ic


def _na_bias(rpb, mask, ir, ic):
    n_layers, n_heads = rpb.shape[:2]
    n_case = ir.shape[0]
    ir_s = ir[:, ::GRID_W, ::GRID_W].reshape(-1)
    ic_s = ic[0, :GRID_W, :GRID_W].reshape(-1)
    oh_r = jnp.asarray(np.eye(2 * NA_ROWS - 1, dtype=np.float32)[ir_s])
    oh_c = jnp.asarray(np.eye(2 * NA_COLS - 1, dtype=np.float32)[ic_s].T)
    b = jnp.einsum("xa,lhab,bp->lhxp", oh_r, rpb.astype(F32), oh_c,
                   precision=lax.Precision.HIGHEST)
    b = b.reshape(n_layers, n_heads, n_case, NA_QROWS, NA_KROWS, GRID_W, GRID_W)
    b = b.transpose(0, 2, 1, 3, 5, 4, 6).reshape(
        n_layers, n_case, n_heads, NA_QROWS * GRID_W, NA_KROWS * GRID_W)
    return jnp.where(mask[None, :, None], b, NEG_INF)


def _na_attention(main3, bias, kr0s, case_ids):
    n_seq, seq_len, _ = main3.shape
    tq = NA_QROWS * GRID_W * NA_SUB_BLOCKS
    grid = (n_seq, seq_len // tq)
    return pl.pallas_call(
        _na_kernel,
        grid_spec=pltpu.PrefetchScalarGridSpec(
            num_scalar_prefetch=2,
            grid=grid,
            in_specs=[
                pl.BlockSpec((1, tq, CHUNK), lambda s, i, kr, cs: (s, i, 0)),
                pl.BlockSpec((1, seq_len, CHUNK), lambda s, i, kr, cs: (s, 0, 1)),
                pl.BlockSpec((1, seq_len, CHUNK), lambda s, i, kr, cs: (s, 0, 2)),
                pl.BlockSpec(bias.shape, lambda s, i, kr, cs: (0, 0, 0, 0),
                             pipeline_mode=pl.Buffered(1)),
            ],
            out_specs=pl.BlockSpec((1, tq, CHUNK), lambda s, i, kr, cs: (s, i, 0)),
        ),
        out_shape=jax.ShapeDtypeStruct((n_seq, seq_len, CHUNK), BF16),
        compiler_params=_cparams(2),
        name="mixer_a",
    )(jnp.asarray(kr0s), jnp.asarray(case_ids), main3, main3, main3, bias)


def _band_kernel(*refs, window, n_kv, rep, length, has_sink, want_lse):
    refs = list(refs)
    sink_ref = refs.pop(0) if has_sink else None
    q_ref, k_ref, v_ref = refs[:3]
    o_ref = refs[3]
    lse_ref = refs[4] if want_lse else None
    tq = 2 * window
    kw = tq + 2 * window
    i = pl.program_id(1)
    for sb in range(q_ref.shape[1] // tq):
        rows = slice(sb * tq, (sb + 1) * tq)
        q0 = i * q_ref.shape[1] + sb * tq
        ks = pl.multiple_of(jnp.clip(q0 - window, 0, length - kw), 64)
        q = q_ref[0, rows, :]
        k = k_ref[0, pl.ds(ks, kw), :]
        v = v_ref[0, pl.ds(ks, kw), :]
        qpos = q0 + lax.broadcasted_iota(I32, (tq, kw), 0)
        kpos = ks + lax.broadcasted_iota(I32, (tq, kw), 1)
        valid = jnp.abs(qpos - kpos) <= window
        outs, lses = [], []
        for h in range(n_kv * rep):
            g = h // rep
            qs = slice(h * HEAD_DIM, (h + 1) * HEAD_DIM)
            kvs = slice(g * HEAD_DIM, (g + 1) * HEAD_DIM)
            s = lax.dot_general(q[:, qs], k[:, kvs], (((1,), (1,)), ((), ())),
                                preferred_element_type=F32)
            s = jnp.where(valid, s, NEG_INF)
            m = jnp.max(s, axis=-1, keepdims=True)
            if has_sink:
                m = jnp.maximum(m, sink_ref[h])
            e = jnp.exp(s - m)
            den = jnp.sum(e, axis=-1, keepdims=True)
            if has_sink:
                den = den + jnp.exp(sink_ref[h] - m)
            o = jnp.dot(e.astype(BF16), v[:, kvs], preferred_element_type=F32)
            outs.append(o / den)
            if want_lse:
                lses.append(jnp.broadcast_to(m + jnp.log(den), (tq, HEAD_DIM)))
        o_ref[0, rows, :] = jnp.concatenate(outs, axis=-1).astype(BF16)
        if want_lse:
            lse_ref[0, rows, :] = jnp.concatenate(lses, axis=-1)


def _band_attention(src3, q_blk, k_blk, v_blk, kv_width, window, n_kv, rep, sink=None,
                    want_lse=False):
    n_seq, length, _ = src3.shape
    tq = min(BAND_STEP_ROWS, length)
    qw = n_kv * rep * HEAD_DIM
    kern = functools.partial(_band_kernel, window=window, n_kv=n_kv, rep=rep, length=length,
                             has_sink=sink is not None, want_lse=want_lse)
    in_specs = [
        pl.BlockSpec((1, tq, qw), lambda s, i: (s, i, q_blk)),
        pl.BlockSpec((1, length, kv_width), lambda s, i: (s, 0, k_blk)),
        pl.BlockSpec((1, length, kv_width), lambda s, i: (s, 0, v_blk)),
    ]
    args = [src3, src3, src3]
    if sink is not None:
        in_specs = [pl.BlockSpec(memory_space=pltpu.SMEM)] + in_specs
        args = [sink.astype(F32)] + args
    o_spec = pl.BlockSpec((1, tq, qw), lambda s, i: (s, i, 0))
    out_specs = [o_spec]
    out_shape = [jax.ShapeDtypeStruct((n_seq, length, qw), BF16)]
    if want_lse:
        out_specs.append(o_spec)
        out_shape.append(jax.ShapeDtypeStruct((n_seq, length, qw), F32))
    res = pl.pallas_call(
        kern,
        grid=(n_seq, length // tq),
        in_specs=in_specs,
        out_specs=out_specs,
        out_shape=out_shape,
        compiler_params=_cparams(2),
        name="mixer_band",
    )(*args)
    return res


def _diff_kernel(lam_ref, q_ref, kt_ref, vx_ref, g_ref, o_ref, q_scr, m_scr, acc_scr, *, seq_len):
    tk = min(DIFF_TK, seq_len)
    n_sub = 2 * DIFF_HEADS
    q = q_ref[0]
    for j in range(n_sub):
        q_scr[j] = q[:, j * DIFF_QK_DIM:(j + 1) * DIFF_QK_DIM]
    acc_scr[...] = jnp.zeros(acc_scr.shape, F32)
    lam = lam_ref[0]
    post = lam_ref[1]
    bound = lam_ref[2]

    def operands(c, j):
        c0 = pl.multiple_of(c * tk, tk)
        kt = kt_ref[0, j * DIFF_QK_DIM:(j + 1) * DIFF_QK_DIM, pl.ds(c0, tk)]
        vv = vx_ref[0, pl.ds(c0, tk), (j // 2) * V7X_LANES:(j // 2 + 1) * V7X_LANES]
        return kt, vv

    @pl.when(bound <= DIFF_FAST_BOUND)
    def _():
        def body(c, carry):
            for j in range(n_sub):
                kt, vv = operands(c, j)
                s = jnp.dot(q_scr[j], kt, preferred_element_type=F32)
                p = jnp.exp2(s - bound).astype(BF16)
                acc_scr[j] += jnp.dot(p, vv, preferred_element_type=F32)
            return carry

        lax.fori_loop(0, seq_len // tk, body, 0)

    @pl.when(bound > DIFF_FAST_BOUND)
    def _():
        m_scr[...] = jnp.full(m_scr.shape, NEG_INF, F32)

        def body(c, carry):
            for j in range(n_sub):
                kt, vv = operands(c, j)
                s = jnp.dot(q_scr[j], kt, preferred_element_type=F32)
                mi = m_scr[j]
                mn = jnp.maximum(mi, jnp.max(s, axis=-1, keepdims=True))
                p = jnp.exp2(s - mn)
                alpha = jnp.exp2(mi - mn)
                acc_scr[j] = alpha * acc_scr[j] + jnp.dot(p.astype(BF16), vv,
                                                          preferred_element_type=F32)
                m_scr[j] = mn
            return carry

        lax.fori_loop(0, seq_len // tk, body, 0)

    outs = []
    for h in range(DIFF_HEADS):
        a0 = acc_scr[2 * h]
        a1 = acc_scr[2 * h + 1]
        od = a0[:, :HEAD_DIM] / a0[:, HEAD_DIM:] - lam * (a1[:, :HEAD_DIM] / a1[:, HEAD_DIM:])
        ms = jnp.mean(od * od, axis=-1, keepdims=True)
        outs.append(od * lax.rsqrt(ms + NORM_EPS) * g_ref[...] * post)
    o_ref[0] = jnp.concatenate(outs, axis=-1).astype(BF16)


def _diff_attention(main3, kdt, vdx3, lam_post, subln):
    n_seq, seq_len, _ = main3.shape
    tq = DIFF_TQ
    kern = functools.partial(_diff_kernel, seq_len=seq_len)
    return pl.pallas_call(
        kern,
        grid=(n_seq, seq_len // tq),
        in_specs=[
            pl.BlockSpec(memory_space=pltpu.SMEM),
            pl.BlockSpec((1, tq, CHUNK), lambda s, i: (s, i, MAIN_COLS // CHUNK - 1)),
            pl.BlockSpec((1, CHUNK, seq_len), lambda s, i: (s, 0, 0),
                         pipeline_mode=pl.Buffered(1)),
            pl.BlockSpec((1, seq_len, 2 * CHUNK), lambda s, i: (s, 0, 0),
                         pipeline_mode=pl.Buffered(1)),
            pl.BlockSpec((1, HEAD_DIM), lambda s, i: (0, 0)),
        ],
        out_specs=pl.BlockSpec((1, tq, CHUNK), lambda s, i: (s, i, 0)),
        out_shape=jax.ShapeDtypeStruct((n_seq, seq_len, CHUNK), BF16),
        scratch_shapes=[
            pltpu.VMEM((2 * DIFF_HEADS, tq, DIFF_QK_DIM), BF16),
            pltpu.VMEM((2 * DIFF_HEADS, tq, 1), F32),
            pltpu.VMEM((2 * DIFF_HEADS, tq, V7X_LANES), F32),
        ],
        compiler_params=_cparams(2),
        name="mixer_d",
    )(lam_post, main3, kdt, vdx3, subln)


def _out_proj_kernel(h_ref, oa_ref, ob_ref, oc0_ref, oc1_ref, oc2_ref, l0_ref, l1_ref, l2_ref,
                     od_ref, w_ref, fn_ref, wr_ref, h1_ref, hn_ref, aff_ref, *scratch):
    tm = h_ref.shape[0]

    def token_major(ref, scr):
        dil = ref.shape[1]
        halves = []
        for half_i in range(CHUNK // V7X_LANES):
            lanes = slice(half_i * V7X_LANES, (half_i + 1) * V7X_LANES)
            for r in range(dil):
                scr.at[half_i][pl.ds(r, tm // dil, stride=dil), :] = ref[0, r, :, lanes].astype(F32)
            halves.append(scr[half_i])
        return jnp.concatenate(halves, axis=-1)

    l0 = l0_ref[...]
    l1 = token_major(l1_ref, scratch[0])
    l2 = token_major(l2_ref, scratch[1])
    mx = jnp.maximum(jnp.maximum(l0, l1), l2)
    e0, e1, e2 = jnp.exp(l0 - mx), jnp.exp(l1 - mx), jnp.exp(l2 - mx)
    den = e0 + e1 + e2
    oc = ((e0 / den) * oc0_ref[...].astype(F32) + (e1 / den) * token_major(oc1_ref, scratch[2])
          + (e2 / den) * token_major(oc2_ref, scratch[3]))
    acc = h_ref[...]
    acc = acc + jnp.dot(oa_ref[...], w_ref[0 * CHUNK:1 * CHUNK, :], preferred_element_type=F32)
    acc = acc + jnp.dot(ob_ref[...], w_ref[1 * CHUNK:2 * CHUNK, :], preferred_element_type=F32)
    acc = acc + jnp.dot(oc.astype(BF16), w_ref[2 * CHUNK:3 * CHUNK, :], preferred_element_type=F32)
    acc = acc + jnp.dot(od_ref[...], w_ref[3 * CHUNK:4 * CHUNK, :], preferred_element_type=F32)
    h1_ref[...] = acc
    ms = jnp.mean(acc * acc, axis=-1, keepdims=True)
    hn = (acc * lax.rsqrt(ms + NORM_EPS) * fn_ref[...]).astype(BF16)
    hn_ref[...] = hn
    logits = jnp.dot(hn, wr_ref[...], preferred_element_type=F32)
    m = jnp.max(logits, axis=-1, keepdims=True)
    e = jnp.exp(logits - m)
    aff_ref[...] = e / jnp.sum(e, axis=-1, keepdims=True)


def _out_proj(h2, oa, ob, ocs, lses, od, w_out, ffn_norm, w_router, tm, seq_len):
    n = h2.shape[0]
    nblk = seq_len // tm
    row = lambda w: pl.BlockSpec((tm, w), lambda i: (i, 0))
    const = lambda r, c: pl.BlockSpec((r, c), lambda i: (0, 0))
    dil = lambda g: _dilated_spec(DIL_CONFIGS[g][1], tm, CHUNK, nblk)
    return pl.pallas_call(
        _out_proj_kernel,
        grid=(n // tm,),
        in_specs=[row(D_MODEL), row(CHUNK), row(CHUNK), row(CHUNK), dil(1), dil(2),
                  row(CHUNK), dil(1), dil(2), row(CHUNK),
                  const(D_MODEL, D_MODEL), const(1, D_MODEL), const(D_MODEL, N_EXPERTS)],
        out_specs=[row(D_MODEL), row(D_MODEL), row(N_EXPERTS)],
        out_shape=[
            jax.ShapeDtypeStruct((n, D_MODEL), F32),
            jax.ShapeDtypeStruct((n, D_MODEL), BF16),
            jax.ShapeDtypeStruct((n, N_EXPERTS), F32),
        ],
        scratch_shapes=[pltpu.VMEM((CHUNK // V7X_LANES, tm, V7X_LANES), F32)] * 4,
        compiler_params=_cparams(1),
        name="out_proj",
    )(h2, oa, ob, ocs[0], ocs[1], ocs[2], lses[0], lses[1], lses[2], od, w_out, ffn_norm,
      w_router)


def _select_kernel(aff_ref, posm_ref, pref_ref, *, cap, n_tok):
    def bits():
        return lax.bitcast_convert_type(aff_ref[...], I32)

    def bit_body(i, thr):
        cand = thr | jnp.left_shift(jnp.int32(1), 30 - i)
        cnt = jnp.sum(jnp.where(bits() >= cand, 1.0, 0.0), axis=-1, keepdims=True)
        return jnp.where(cnt >= cap, cand, thr)

    thr = lax.fori_loop(0, 31, bit_body, jnp.zeros((N_EXPERTS, 1), I32))
    cnt_gt = jnp.sum(jnp.where(bits() > thr, 1.0, 0.0), axis=-1, keepdims=True)
    need = cap - cnt_gt
    blk = SLOT_BLK
    r = lax.broadcasted_iota(I32, (blk, blk), 0)
    c = lax.broadcasted_iota(I32, (blk, blk), 1)
    upper = jnp.where(r < c, 1.0, 0.0).astype(BF16)

    def blk_body(b, carry):
        ceq, csel = carry
        b0 = pl.multiple_of(b * blk, blk)
        xb = lax.bitcast_convert_type(aff_ref[:, pl.ds(b0, blk)], I32)
        gt = xb > thr
        eq = xb == thr
        eqf = jnp.where(eq, 1.0, 0.0)
        rank = ceq + jnp.dot(eqf.astype(BF16), upper, preferred_element_type=F32)
        sel = gt | (eq & (rank < need))
        self_ = jnp.where(sel, 1.0, 0.0)
        pref = csel + jnp.dot(self_.astype(BF16), upper, preferred_element_type=F32)
        prefi = pref.astype(I32)
        pref_ref[:, pl.ds(b0, blk)] = prefi
        posm_ref[:, pl.ds(b0, blk)] = jnp.where(sel, prefi, -1)
        return (ceq + jnp.sum(eqf, axis=-1, keepdims=True),
                csel + jnp.sum(self_, axis=-1, keepdims=True))

    zero = jnp.zeros((N_EXPERTS, 1), F32)
    lax.fori_loop(0, n_tok // blk, blk_body, (zero, zero))


def _select(aff_t, cap):
    n_tok = aff_t.shape[1]
    kern = functools.partial(_select_kernel, cap=cap, n_tok=n_tok)
    full = pl.BlockSpec((N_EXPERTS, n_tok), lambda i: (0, 0))
    return pl.pallas_call(
        kern,
        grid=(1,),
        in_specs=[full],
        out_specs=[full, full],
        out_shape=[jax.ShapeDtypeStruct((N_EXPERTS, n_tok), I32)] * 2,
        compiler_params=_cparams(1),
        name="select",
    )(aff_t)


_VALID, _FIRST, _LAST = 1, 2, 4


_NO_HIT = 1 << 30


def _block_windows(pref, cap, tok_blk, slot_blk):
    n_exp = pref.shape[0]
    nw = cap // slot_blk
    off = pref[:, ::tok_blk]
    end = jnp.concatenate([off[:, 1:], jnp.full((n_exp, 1), cap, I32)], axis=1)
    cnt = end - off
    w_lo = jnp.minimum(off // slot_blk, nw - 1)
    w_hi = jnp.where(cnt > 0, (end - 1) // slot_blk, w_lo)
    return cnt, w_lo, w_hi


def _search(starts, idx):
    return (jnp.sum(starts[None, :] <= idx.reshape(-1, 1), axis=1) - 1).astype(I32).reshape(idx.shape)


def _gather_plan(pref, cap):
    n_exp, n_tok = pref.shape
    nb = n_tok // GATHER_TOK_BLK
    nw = cap // SLOT_BLK
    _, w_lo, w_hi = _block_windows(pref, cap, GATHER_TOK_BLK, SLOT_BLK)
    s_g = nb + nw
    nst = w_hi - w_lo + 1
    start = jnp.cumsum(nst, axis=1) - nst
    total = jnp.sum(nst, axis=1, keepdims=True)
    s_idx = jnp.arange(s_g, dtype=I32)[None, :]
    s_eff = jnp.minimum(s_idx, total - 1)
    b_s = jax.vmap(_search)(start, s_eff)
    w_s = jnp.take_along_axis(w_lo, b_s, axis=1) + (s_eff - jnp.take_along_axis(start, b_s, axis=1))
    valid = s_idx < total
    prev_w = jnp.concatenate([jnp.full((n_exp, 1), -1, I32), w_s[:, :-1]], axis=1)
    next_w = jnp.concatenate([w_s[:, 1:], jnp.full((n_exp, 1), -1, I32)], axis=1)
    first = valid & (w_s != prev_w)
    last = valid & ((w_s != next_w) | (s_idx == total - 1))
    flag = valid * _VALID + first * _FIRST + last * _LAST
    return (b_s.reshape(-1), w_s.reshape(-1).astype(I32), flag.reshape(-1).astype(I32)), s_g


def _combine_plan(pref, cap):
    n_exp, n_tok = pref.shape
    nb = n_tok // COMBINE_TOK_BLK
    nw = cap // COMBINE_SLOT_BLK
    grp = COMBINE_GROUP
    cnt, w_lo, w_hi = _block_windows(pref, cap, COMBINE_TOK_BLK, COMBINE_SLOT_BLK)
    cnt_t, wlo_t, whi_t = cnt.T, w_lo.T, w_hi.T
    n_win = jnp.where(cnt_t > 0, whi_t - wlo_t + 1, 0)
    win_start = (jnp.cumsum(n_win.reshape(-1)) - n_win.reshape(-1)).astype(I32)
    per_blk = jnp.sum(n_win, axis=1)
    blk_first_win = (jnp.cumsum(per_blk) - per_blk).astype(I32)
    steps = jnp.maximum((per_blk + grp - 1) // grp, 1)
    step_start = (jnp.cumsum(steps) - steps).astype(I32)
    total = jnp.sum(steps)
    s_c = (n_exp * (nb + nw) + grp - 1) // grp + nb
    s_idx = jnp.arange(s_c, dtype=I32)
    s_eff = jnp.minimum(s_idx, total - 1)
    cb = _search(step_start, s_eff)
    local = s_eff - step_start[cb]
    q = local[:, None] * grp + jnp.arange(grp, dtype=I32)[None, :]
    live = (q < per_blk[cb][:, None]) & (s_idx < total)[:, None]
    win = jnp.where(live, blk_first_win[cb][:, None] + q, 0)
    pair = _search(win_start, win.reshape(-1)).reshape(win.shape)
    ce = pair % n_exp
    cw = wlo_t.reshape(-1)[pair] + (win - win_start[pair])
    cw = jnp.clip(cw, 0, nw - 1)
    blk = (ce * nw + cw).astype(I32)
    base = jnp.where(live, cw * COMBINE_SLOT_BLK, _NO_HIT).astype(I32)
    valid = s_idx < total
    prev_b = jnp.concatenate([jnp.full((1,), -1, I32), cb[:-1]])
    next_b = jnp.concatenate([cb[1:], jnp.full((1,), -1, I32)])
    first = valid & (cb != prev_b)
    last = valid & ((cb != next_b) | (s_idx == total - 1))
    flag = (valid * _VALID + first * _FIRST + last * _LAST).astype(I32)
    return (cb, ce.reshape(-1).astype(I32), blk.reshape(-1), base.reshape(-1), flag), s_c


def _gather_kernel(gb_ref, gw_ref, gf_ref, posm_ref, g_ref, hn_ref, xe_ref, gs_ref, acc_ref,
                   gacc_ref, *, n_steps):
    e = pl.program_id(0)
    s = pl.program_id(1)
    flag = gf_ref[e * n_steps + s]
    w = gw_ref[e * n_steps + s]

    @pl.when((flag & _FIRST) != 0)
    def _():
        acc_ref[...] = jnp.zeros_like(acc_ref)
        gacc_ref[...] = jnp.zeros_like(gacc_ref)

    @pl.when((flag & _VALID) != 0)
    def _():
        rel = posm_ref[0] - w * SLOT_BLK
        hit = lax.broadcasted_iota(I32, (SLOT_BLK, GATHER_TOK_BLK), 0) == rel
        onehot = jnp.where(hit, 1.0, 0.0).astype(BF16)
        acc_ref[...] += jnp.dot(onehot, hn_ref[...], preferred_element_type=F32)
        gacc_ref[...] += jnp.sum(jnp.where(hit, g_ref[0], 0.0), axis=-1, keepdims=True)

    @pl.when((flag & _LAST) != 0)
    def _():
        xe_ref[0] = acc_ref[...].astype(BF16)
        gs_ref[0] = gacc_ref[...]


def _gather(plan, s_g, posm3, g3, hn, cap):
    gb, gw, gf = plan
    n_tok = hn.shape[0]
    kern = functools.partial(_gather_kernel, n_steps=s_g)
    tok = lambda e, s, gb, gw, gf: (e, 0, gb[e * s_g + s])
    slot = lambda e, s, gb, gw, gf: (e, gw[e * s_g + s], 0)
    return pl.pallas_call(
        kern,
        grid_spec=pltpu.PrefetchScalarGridSpec(
            num_scalar_prefetch=3,
            grid=(N_EXPERTS, s_g),
            in_specs=[
                pl.BlockSpec((1, 1, GATHER_TOK_BLK), tok),
                pl.BlockSpec((1, 1, GATHER_TOK_BLK), tok),
                pl.BlockSpec((GATHER_TOK_BLK, D_MODEL), lambda e, s, gb, gw, gf: (gb[e * s_g + s], 0)),
            ],
            out_specs=[
                pl.BlockSpec((1, SLOT_BLK, D_MODEL), slot),
                pl.BlockSpec((1, SLOT_BLK, 1), slot),
            ],
            scratch_shapes=[pltpu.VMEM((SLOT_BLK, D_MODEL), F32), pltpu.VMEM((SLOT_BLK, 1), F32)],
        ),
        out_shape=[
            jax.ShapeDtypeStruct((N_EXPERTS, cap, D_MODEL), BF16),
            jax.ShapeDtypeStruct((N_EXPERTS, cap, 1), F32),
        ],
        compiler_params=_cparams(2),
        name="gather",
    )(gb, gw, gf, posm3, g3, hn)


def _experts_kernel(layer_ref, x_ref, gs_ref, wg_ref, wu_ref, wd_ref, y_ref):
    del layer_ref
    x = x_ref[0]
    hg = jnp.dot(x, wg_ref[0, 0], preferred_element_type=F32)
    hu = jnp.dot(x, wu_ref[0, 0], preferred_element_type=F32)
    hdn = (hg / (1.0 + jnp.exp(-hg)) * hu).astype(BF16)
    y = jnp.dot(hdn, wd_ref[0, 0], preferred_element_type=F32)
    y_ref[0] = (y * gs_ref[0]).astype(BF16)


def _experts(layer, xe, gs, w_gate, w_up, w_down):
    n_exp, cap, _ = xe.shape
    tc = min(512, cap)
    wspec = pl.BlockSpec((1, 1, D_MODEL, D_MODEL), lambda e, c, layer: (layer[0], e, 0, 0))
    return pl.pallas_call(
        _experts_kernel,
        grid_spec=pltpu.PrefetchScalarGridSpec(
            num_scalar_prefetch=1,
            grid=(n_exp, cap // tc),
            in_specs=[
                pl.BlockSpec((1, tc, D_MODEL), lambda e, c, layer: (e, c, 0)),
                pl.BlockSpec((1, tc, 1), lambda e, c, layer: (e, c, 0)),
                wspec, wspec, wspec,
            ],
            out_specs=pl.BlockSpec((1, tc, D_MODEL), lambda e, c, layer: (e, c, 0)),
        ),
        out_shape=jax.ShapeDtypeStruct((n_exp, cap, D_MODEL), BF16),
        compiler_params=_cparams(2),
        name="experts",
    )(layer.reshape(1).astype(I32), xe, gs, w_gate, w_up, w_down)


def _combine_kernel(cb_ref, ce_ref, cblk_ref, cbase_ref, cf_ref, h1_ref, posmt_ref, *rest):
    del cb_ref, cblk_ref
    grp = COMBINE_GROUP
    ye_refs = rest[:grp]
    p_ref, pn_ref, wpg_ref, wpp_ref, o_ref, acc_ref = rest[grp:]
    s = pl.program_id(0)
    flag = cf_ref[s]

    @pl.when((flag & _FIRST) != 0)
    def _():
        acc_ref[...] = jnp.zeros_like(acc_ref)

    @pl.when((flag & _VALID) != 0)
    def _():
        pm = posmt_ref[...].astype(F32)
        lane = lax.broadcasted_iota(I32, pm.shape, 1)
        slot_iota = lax.broadcasted_iota(I32, (COMBINE_TOK_BLK, COMBINE_SLOT_BLK), 1)
        hots = []
        for k in range(grp):
            e = ce_ref[s * grp + k]
            col = jnp.sum(jnp.where(lane == e, pm, 0.0), axis=-1, keepdims=True)
            rel = col.astype(I32) - cbase_ref[s * grp + k]
            hots.append(jnp.where(slot_iota == rel, 1.0, 0.0).astype(BF16))
        onehot = jnp.concatenate(hots, axis=1)
        rows = jnp.concatenate([r[0] for r in ye_refs], axis=0)
        acc_ref[...] += jnp.dot(onehot, rows, preferred_element_type=F32)

    @pl.when((flag & _LAST) != 0)
    def _():
        h2 = h1_ref[...] + acc_ref[...]
        ms = jnp.mean(h2 * h2, axis=-1, keepdims=True)
        hn = (h2 * lax.rsqrt(ms + NORM_EPS) * pn_ref[...]).astype(BF16)
        z = jnp.dot(hn, wpg_ref[...], preferred_element_type=F32)
        gate = 1.0 / (1.0 + jnp.exp(-z))
        proj = jnp.dot(p_ref[...].astype(BF16), wpp_ref[...], preferred_element_type=F32)
        o_ref[...] = h2 + gate * proj


def _combine(plan, s_c, h1, posm_t, ye, p2, ple_norm, w_ple_gate, w_ple_proj, cap):
    cb, ce, cblk, cbase, cf = plan
    n_tok = h1.shape[0]
    nw = cap // COMBINE_SLOT_BLK
    grp = COMBINE_GROUP
    ye3 = ye.reshape(N_EXPERTS * nw, COMBINE_SLOT_BLK, D_MODEL)
    tokb = lambda width: pl.BlockSpec((COMBINE_TOK_BLK, width), lambda s, cb, *_: (cb[s], 0))
    const = lambda r, c: pl.BlockSpec((r, c), lambda s, *_: (0, 0))

    def ye_spec(k):
        return pl.BlockSpec((1, COMBINE_SLOT_BLK, D_MODEL),
                            lambda s, cb, ce, cblk, *_: (cblk[s * grp + k], 0, 0))

    return pl.pallas_call(
        _combine_kernel,
        grid_spec=pltpu.PrefetchScalarGridSpec(
            num_scalar_prefetch=5,
            grid=(s_c,),
            in_specs=[tokb(D_MODEL), tokb(N_EXPERTS)] + [ye_spec(k) for k in range(grp)] + [
                tokb(PLE_DIM), const(1, D_MODEL), const(D_MODEL, D_MODEL),
                const(PLE_DIM, D_MODEL)],
            out_specs=tokb(D_MODEL),
            scratch_shapes=[pltpu.VMEM((COMBINE_TOK_BLK, D_MODEL), F32)],
        ),
        out_shape=jax.ShapeDtypeStruct((n_tok, D_MODEL), F32),
        compiler_params=_cparams(1),
        name="combine",
    )(cb, ce, cblk, cbase, cf, h1, posm_t, *([ye3] * grp), p2, ple_norm, w_ple_gate, w_ple_proj)


def _layer(h2, p2, lw, static):
    n_seq, seq_len, tabs, b64, b32, na_plan = static
    n_tok = h2.shape[0]
    tm = min(512, seq_len)
    gains = _in_gains(lw["qk_norm_a"], lw["qk_norm_b"], lw["qk_norm_c"], lw["qk_norm_d"])
    main, cg1, cg2, kdt, vdx = _in_proj(h2, lw["attn_norm"][None, :], lw["w_in"], b64, b32, gains,
                                        tabs, n_seq, seq_len, tm)
    main3 = main.reshape(n_seq, seq_len, MAIN_COLS)

    kr0s, case_ids = na_plan
    oa = _na_attention(main3, lw["na_bias"], kr0s, case_ids)

    (ob,) = _band_attention(main3, 3, 8, 9, V7X_LANES, SW_WINDOW, SW_KV_HEADS,
                            SW_HEADS // SW_KV_HEADS, sink=lw["sink_b"])
    ocs, lses = [], []
    for g_i, (win, dil) in enumerate(DIL_CONFIGS):
        if dil == 1:
            src, cols = main3, (5, 6, 7)
        else:
            src = (cg1, cg2)[g_i - 1].reshape(n_seq * dil, seq_len // dil, 3 * CHUNK)
            cols = (0, 1, 2)
        o_g, l_g = _band_attention(src, *cols, CHUNK, win // (2 * dil), DIL_HEADS, 1,
                                   want_lse=True)
        shape = (n_tok, CHUNK) if dil == 1 else (n_seq, dil, seq_len // dil, CHUNK)
        ocs.append(o_g.reshape(shape))
        lses.append(l_g.reshape(shape))

    lf = lw["lambda_d"].astype(F32)
    lam = jnp.exp(jnp.sum(lf[0] * lf[1])) - jnp.exp(jnp.sum(lf[2] * lf[3])) + lw["lam_init"]
    gd = jnp.abs(lw["qk_norm_d"].astype(F32))
    score_bound = (LOG2_E * DIFF_QK_DIM ** 0.5 * (1.0 + 2.0 ** -6)) * jnp.max(gd[0]) * jnp.max(gd[1])
    lam_post = jnp.stack([lam, 1.0 - lw["lam_init"], score_bound]).astype(F32)
    od = _diff_attention(main3, kdt, vdx.reshape(n_seq, seq_len, 2 * CHUNK), lam_post,
                         lw["subln_d"].astype(F32)[None, :])

    h1, hn, aff = _out_proj(h2, oa.reshape(n_tok, CHUNK), ob.reshape(n_tok, CHUNK), ocs, lses,
                            od.reshape(n_tok, CHUNK), lw["w_out"], lw["ffn_norm"][None, :],
                            lw["w_router"], tm, seq_len)

    cap = EC_CAPACITY_FACTOR * n_tok // N_EXPERTS
    aff_t = aff.T
    posm, pref = _select(aff_t, cap)
    gather_plan, s_g = _gather_plan(pref, cap)
    combine_plan, s_c = _combine_plan(pref, cap)
    xe, gs = _gather(gather_plan, s_g, posm.reshape(N_EXPERTS, 1, n_tok),
                     aff_t.reshape(N_EXPERTS, 1, n_tok), hn, cap)
    ye = _experts(lw["layer"], xe, gs, lw["w_gate"], lw["w_up"], lw["w_down"])
    return _combine(combine_plan, s_c, h1, posm.T, ye, p2, lw["ple_norm"][None, :],
                    lw["w_ple_gate"], lw["w_ple_proj"], cap)


def _trunk(x, p, weights):
    n_seq, seq_len, _ = x.shape
    depth = p.shape[0]
    tabs = _rope_tables(seq_len, HEAD_DIM // 2, 4) + _rope_tables(seq_len, DIFF_QK_DIM // 2, 8)
    kr0s, case_ids, mask, ir, ic = _na_plan(seq_len // GRID_W)
    static = (n_seq, seq_len, tabs, _block_mean_matrix(HEAD_DIM), _block_mean_matrix(DIFF_QK_DIM),
              (kr0s, case_ids))
    stacked = ("w_gate", "w_up", "w_down")
    xs = {k: v for k, v in weights.items() if k not in stacked}
    xs["p"] = p.reshape(depth, n_seq * seq_len, PLE_DIM)
    xs["na_bias"] = _na_bias(weights["rpb_a"], mask, ir, ic)
    xs["layer"] = jnp.arange(depth, dtype=I32)

    def step(h2, lw):
        lw = dict(lw, **{k: weights[k] for k in stacked})
        return _layer(h2, lw["p"], lw, static), None

    h2, _ = lax.scan(step, x.reshape(n_seq * seq_len, D_MODEL), xs)
    return h2.reshape(n_seq, seq_len, D_MODEL)


def kernel(x_prompt, x_sample, p_prompt, p_sample, attn_norm, w_in, rpb_a, qk_norm_a, qk_norm_b, sink_b, qk_norm_c, qk_norm_d, lambda_d, subln_d, w_out, ffn_norm, w_router, w_gate, w_up, w_down, ple_norm, w_ple_gate, w_ple_proj):
    depth = w_in.shape[0]
    lam_init = jnp.asarray([0.8 - 0.6 * math.exp(-0.3 * i) for i in range(depth)], F32)
    weights = dict(
        attn_norm=attn_norm.astype(F32), w_in=jax.vmap(_extend_w_in)(w_in), rpb_a=rpb_a,
        qk_norm_a=qk_norm_a, qk_norm_b=qk_norm_b, sink_b=sink_b, qk_norm_c=qk_norm_c,
        qk_norm_d=qk_norm_d, lambda_d=lambda_d, subln_d=subln_d, w_out=w_out.astype(BF16),
        ffn_norm=ffn_norm.astype(F32), w_router=w_router.astype(BF16), w_gate=w_gate.astype(BF16),
        w_up=w_up.astype(BF16), w_down=w_down.astype(BF16), ple_norm=ple_norm.astype(F32),
        w_ple_gate=w_ple_gate.astype(BF16), w_ple_proj=w_ple_proj.astype(BF16), lam_init=lam_init)
    y_prompt = _trunk(x_prompt, p_prompt, weights)
    y_sample = _trunk(x_sample, p_sample, weights)
    return (y_prompt, y_sample)
```

```python
import functools
import math

import jax
import jax.numpy as jnp
import numpy as np
from jax import lax
from jax.experimental import pallas as pl
from jax.experimental.pallas import tpu as pltpu

F32 = jnp.float32
BF16 = jnp.bfloat16
I32 = jnp.int32

D_MODEL = 1024
HEAD_DIM = 64
GRID_W = 64
NA_HEADS = 4
NA_ROWS = 8
NA_COLS = 16
SW_HEADS = 4
SW_KV_HEADS = 2
SW_WINDOW = 128
DIL_CONFIGS = ((128, 1), (512, 4), (2048, 16))
DIL_HEADS = 4
DIFF_HEADS = 4
DIFF_QK_DIM = HEAD_DIM // 2
ROPE_THETA = 10000.0
N_EXPERTS = 16
EC_CAPACITY_FACTOR = 2
PLE_DIM = 256
NEG_INF = -1e30
NORM_EPS = 1e-6

V7X_LANES = 128
V7X_VMEM_LIMIT = 56 * 1024 * 1024

CHUNK = 256
MAIN_COLS = 9 * CHUNK
N_IN_CHUNKS = 18
IN_PROJ_ROWS = 1024
NA_QROWS = 2
NA_KROWS = NA_QROWS + NA_ROWS - 1
NA_SUB_BLOCKS = 4
BAND_STEP_ROWS = 512
DIFF_TQ = 1024
DIFF_TK = 512
DIFF_FAST_BOUND = 40.0
LOG2_E = math.log2(math.e)
SLOT_BLK = 256
GATHER_TOK_BLK = 2048
COMBINE_TOK_BLK = 512
COMBINE_SLOT_BLK = 128
COMBINE_GROUP = 8


def _cparams(n_axes):
    return pltpu.CompilerParams(
        dimension_semantics=("arbitrary",) * n_axes, vmem_limit_bytes=V7X_VMEM_LIMIT)


_IN_CHUNK_KIND = (
    (64, 0), (64, 0), (0, 0),
    (64, 32), (64, 32),
    (64, 32), (64, 32), (64, 32),
    (64, 32), (64, 32), (64, 32),
    (0, 0), (0, 0), (0, 0),
    (32, 16),
    (32, 16),
    (0, 0), (0, 0),
)
_KBVB_CHUNK = 4
_IN_CHUNK_DEST = (
    ("main", 0), ("main", 1), ("main", 2), ("main", 3), ("main", 4),
    ("main", 5), ("dil", (0, 0)), ("dil", (1, 0)),
    ("main", 6), ("dil", (0, 1)), ("dil", (1, 1)),
    ("main", 7), ("dil", (0, 2)), ("dil", (1, 2)),
    ("main", 8), ("kdt", 0), ("vdx", 0), ("vdx", 1),
)


def _in_proj_kernel(h_ref, gn_ref, w_ref, b64_ref, b32_ref, gains_ref, cs64_ref, sn64_ref,
                    cs32_ref, sn32_ref, main_ref, cg1_ref, cg2_ref, kdt_ref, vdx_ref, row_scr):
    x = h_ref[...]
    ms = jnp.mean(x * x, axis=-1, keepdims=True)
    xn = (x * lax.rsqrt(ms + NORM_EPS) * gn_ref[...]).astype(BF16)
    tm = x.shape[0]
    lane = lax.broadcasted_iota(I32, (tm, CHUNK), 1)
    def project(c):
        return jnp.dot(xn, w_ref[:, c * CHUNK:(c + 1) * CHUNK], preferred_element_type=F32)

    y_next = project(0)
    for c in range(N_IN_CHUNKS):
        hd, half = _IN_CHUNK_KIND[c]
        y = y_next
        if c + 1 < N_IN_CHUNKS:
            y_next = project(c + 1)
        out = y
        if hd:
            bmat = b64_ref[...] if hd == 64 else b32_ref[...]
            msq = jnp.dot((y * y).astype(BF16), bmat, preferred_element_type=F32)
            out = y * lax.rsqrt(msq + NORM_EPS) * gains_ref[c:c + 1, :]
        if half:
            cs = cs64_ref[...] if half == 32 else cs32_ref[...]
            sn = sn64_ref[...] if half == 32 else sn32_ref[...]
            fwd = pltpu.roll(out, half, 1)
            bwd = pltpu.roll(out, CHUNK - half, 1)
            first = (lane & (2 * half - 1)) < half
            out = out * cs + jnp.where(first, bwd, fwd) * sn
        if c == _KBVB_CHUNK:
            out = jnp.where(lane < CHUNK // 2, out, y)
        kind, where = _IN_CHUNK_DEST[c]
        if kind == "main":
            main_ref[:, where * CHUNK:(where + 1) * CHUNK] = out.astype(BF16)
        elif kind == "dil":
            group, part = where
            dst = (cg1_ref, cg2_ref)[group]
            dil = dst.shape[1]
            for half_i in range(CHUNK // V7X_LANES):
                lanes = slice(half_i * V7X_LANES, (half_i + 1) * V7X_LANES)
                row_scr[half_i] = out[:, lanes]
                for r in range(dil):
                    col0 = part * CHUNK + half_i * V7X_LANES
                    dst[0, r, :, col0:col0 + V7X_LANES] = row_scr.at[half_i][
                        pl.ds(r, tm // dil, stride=dil), :].astype(BF16)
        elif kind == "kdt":
            kdt_ref[0] = out.T.astype(BF16)
        else:
            ones = jnp.where((lane & (V7X_LANES - 1)) >= HEAD_DIM, 1.0, 0.0)
            vdx_ref[:, where * CHUNK:(where + 1) * CHUNK] = (out + ones).astype(BF16)


def _dilated_spec(dil, tm, width, nblk):
    return pl.BlockSpec((1, dil, tm // dil, width), lambda i: (i // nblk, 0, i % nblk, 0))


def _in_proj(h2, gn, w_ext, b64, b32, gains, tabs, n_seq, seq_len, tm):
    n = h2.shape[0]
    nblk = seq_len // tm
    d1, d2 = DIL_CONFIGS[1][1], DIL_CONFIGS[2][1]
    cs64, sn64, cs32, sn32 = tabs
    const = lambda i: (0, 0)
    tab = lambda i: (i % nblk, 0)
    return pl.pallas_call(
        _in_proj_kernel,
        grid=(n // tm,),
        in_specs=[
            pl.BlockSpec((tm, D_MODEL), lambda i: (i, 0)),
            pl.BlockSpec((1, D_MODEL), const),
            pl.BlockSpec((D_MODEL, N_IN_CHUNKS * CHUNK), const),
            pl.BlockSpec((CHUNK, CHUNK), const),
            pl.BlockSpec((CHUNK, CHUNK), const),
            pl.BlockSpec((N_IN_CHUNKS, CHUNK), const),
            pl.BlockSpec((tm, CHUNK), tab),
            pl.BlockSpec((tm, CHUNK), tab),
            pl.BlockSpec((tm, CHUNK), tab),
            pl.BlockSpec((tm, CHUNK), tab),
        ],
        out_specs=[
            pl.BlockSpec((tm, MAIN_COLS), lambda i: (i, 0)),
            _dilated_spec(d1, tm, 3 * CHUNK, nblk),
            _dilated_spec(d2, tm, 3 * CHUNK, nblk),
            pl.BlockSpec((1, CHUNK, tm), lambda i: (i // nblk, 0, i % nblk)),
            pl.BlockSpec((tm, 2 * CHUNK), lambda i: (i, 0)),
        ],
        out_shape=[
            jax.ShapeDtypeStruct((n, MAIN_COLS), BF16),
            jax.ShapeDtypeStruct((n_seq, d1, seq_len // d1, 3 * CHUNK), BF16),
            jax.ShapeDtypeStruct((n_seq, d2, seq_len // d2, 3 * CHUNK), BF16),
            jax.ShapeDtypeStruct((n_seq, CHUNK, seq_len), BF16),
            jax.ShapeDtypeStruct((n, 2 * CHUNK), BF16),
        ],
        scratch_shapes=[pltpu.VMEM((CHUNK // V7X_LANES, tm, V7X_LANES), F32)],
        compiler_params=_cparams(1),
        name="in_proj",
    )(h2, gn, w_ext, b64, b32, gains, cs64, sn64, cs32, sn32)


def _rope_tables(seq_len, half, reps):
    inv_freq = ROPE_THETA ** (-jnp.arange(half, dtype=F32) / half)
    ang = jnp.arange(seq_len, dtype=F32)[:, None] * inv_freq[None, :]
    cos, sin = jnp.cos(ang), jnp.sin(ang)
    cs = jnp.tile(jnp.concatenate([cos, cos], axis=-1), (1, reps))
    sn = jnp.tile(jnp.concatenate([-sin, sin], axis=-1), (1, reps))
    return cs, sn


def _block_mean_matrix(head):
    blocks = CHUNK // head
    return jnp.asarray(np.kron(np.eye(blocks), np.full((head, head), 1.0 / head)), BF16)


def _extend_w_in(w_in):
    vd = w_in[:, 4096:4352].reshape(D_MODEL, DIFF_HEADS, HEAD_DIM)
    vdx = jnp.concatenate([vd, jnp.zeros_like(vd)], axis=-1).reshape(D_MODEL, 2 * CHUNK)
    return jnp.concatenate([w_in[:, :4096], vdx], axis=-1).astype(BF16)


def _in_gains(qk_a, qk_b, qk_c, qk_d):
    one = jnp.ones((CHUNK,), F32)
    s64 = HEAD_DIM ** -0.5
    s32 = DIFF_QK_DIM ** -0.5
    t4 = lambda g: jnp.tile(g.astype(F32), 4)
    rows = [
        t4(qk_a[0]) * s64, t4(qk_a[1]), one,
        t4(qk_b[0]) * s64, jnp.concatenate([jnp.tile(qk_b[1].astype(F32), 2), jnp.ones((128,), F32)]),
        t4(qk_c[0]) * s64, t4(qk_c[0]) * s64, t4(qk_c[0]) * s64,
        t4(qk_c[1]), t4(qk_c[1]), t4(qk_c[1]),
        one, one, one,
        jnp.tile(qk_d[0].astype(F32), 8) * (s32 * LOG2_E),
        jnp.tile(qk_d[1].astype(F32), 8),
        one, one,
    ]
    return jnp.stack(rows)


def _na_kernel(kr0_ref, case_ref, q_ref, k_ref, v_ref, bias_ref, o_ref):
    i = pl.program_id(1)
    tq = NA_QROWS * GRID_W
    nk = NA_KROWS * GRID_W
    n_sub = q_ref.shape[1] // tq

    def key_start(sb):
        return pl.multiple_of(kr0_ref[i * n_sub + sb] * GRID_W, GRID_W)

    def scores(sb, h):
        sl = slice(h * HEAD_DIM, (h + 1) * HEAD_DIM)
        q = q_ref[0, sb * tq:(sb + 1) * tq, sl]
        k = k_ref[0, pl.ds(key_start(sb), nk), sl]
        return lax.dot_general(q, k, (((1,), (1,)), ((), ())), preferred_element_type=F32)

    chains = [(sb, h) for sb in range(n_sub) for h in range(NA_HEADS)]
    s_next = scores(*chains[0])
    outs = []
    for idx, (sb, h) in enumerate(chains):
        s = s_next
        if idx + 1 < len(chains):
            s_next = scores(*chains[idx + 1])
        s = s + bias_ref[case_ref[i * n_sub + sb], h]
        m = jnp.max(s, axis=-1, keepdims=True)
        e = jnp.exp(s - m)
        den = jnp.sum(e, axis=-1, keepdims=True)
        v = v_ref[0, pl.ds(key_start(sb), nk), h * HEAD_DIM:(h + 1) * HEAD_DIM]
        o = jnp.dot(e.astype(BF16), v, preferred_element_type=F32)
        outs.append(o / den)
        if h == NA_HEADS - 1:
            o_ref[0, sb * tq:(sb + 1) * tq, :] = jnp.concatenate(outs, axis=-1).astype(BF16)
            outs = []


def _na_plan(rows):
    kr, kc = NA_ROWS, NA_COLS
    nblk = rows // NA_QROWS
    kr0s, case_ids, cases = [], [], {}
    tables = []
    qi = np.arange(NA_QROWS * GRID_W)
    ki = np.arange(NA_KROWS * GRID_W)
    qr_l, qc = qi // GRID_W, qi % GRID_W
    kr_l, kcol = ki // GRID_W, ki % GRID_W
    cstart = np.clip(qc - kc // 2, 0, GRID_W - kc)
    for b in range(nblk):
        r0 = b * NA_QROWS
        kr0 = int(np.clip(r0 - kr // 2, 0, rows - NA_KROWS))
        rstart = np.clip(r0 + np.arange(NA_QROWS) - kr // 2, 0, rows - kr)
        key = (r0 - kr0, tuple(int(t) for t in rstart - kr0))
        if key not in cases:
            cases[key] = len(cases)
            q_row = r0 + qr_l
            k_row = kr0 + kr_l
            rs = rstart[qr_l]
            in_row = (k_row[None, :] >= rs[:, None]) & (k_row[None, :] < rs[:, None] + kr)
            in_col = (kcol[None, :] >= cstart[:, None]) & (kcol[None, :] < cstart[:, None] + kc)
            ir = np.clip(k_row[None, :] - q_row[:, None] + NA_ROWS - 1, 0, 2 * NA_ROWS - 2)
            ic = np.clip(kcol[None, :] - qc[:, None] + NA_COLS - 1, 0, 2 * NA_COLS - 2)
            tables.append((in_row & in_col, ir, ic))
        kr0s.append(kr0)
        case_ids.append(cases[key])
    mask = np.stack([t[0] for t in tables])
    ir = np.stack([t[1] for t in tables])
    ic = np.stack([t[2] for t in tables])
    return np.asarray(kr0s, np.int32), np.asarray(case_ids, np.int32), mask, ir, ic


def _na_bias(rpb, mask, ir, ic):
    n_layers, n_heads = rpb.shape[:2]
    n_case = ir.shape[0]
    ir_s = ir[:, ::GRID_W, ::GRID_W].reshape(-1)
    ic_s = ic[0, :GRID_W, :GRID_W].reshape(-1)
    oh_r = jnp.asarray(np.eye(2 * NA_ROWS - 1, dtype=np.float32)[ir_s])
    oh_c = jnp.asarray(np.eye(2 * NA_COLS - 1, dtype=np.float32)[ic_s].T)
    b = jnp.einsum("xa,lhab,bp->lhxp", oh_r, rpb.astype(F32), oh_c,
                   precision=lax.Precision.HIGHEST)
    b = b.reshape(n_layers, n_heads, n_case, NA_QROWS, NA_KROWS, GRID_W, GRID_W)
    b = b.transpose(0, 2, 1, 3, 5, 4, 6).reshape(
        n_layers, n_case, n_heads, NA_QROWS * GRID_W, NA_KROWS * GRID_W)
    return jnp.where(mask[None, :, None], b, NEG_INF)


def _na_attention(main3, bias, kr0s, case_ids):
    n_seq, seq_len, _ = main3.shape
    tq = NA_QROWS * GRID_W * NA_SUB_BLOCKS
    grid = (n_seq, seq_len // tq)
    return pl.pallas_call(
        _na_kernel,
        grid_spec=pltpu.PrefetchScalarGridSpec(
            num_scalar_prefetch=2,
            grid=grid,
            in_specs=[
                pl.BlockSpec((1, tq, CHUNK), lambda s, i, kr, cs: (s, i, 0)),
                pl.BlockSpec((1, seq_len, CHUNK), lambda s, i, kr, cs: (s, 0, 1)),
                pl.BlockSpec((1, seq_len, CHUNK), lambda s, i, kr, cs: (s, 0, 2)),
                pl.BlockSpec(bias.shape, lambda s, i, kr, cs: (0, 0, 0, 0),
                             pipeline_mode=pl.Buffered(1)),
            ],
            out_specs=pl.BlockSpec((1, tq, CHUNK), lambda s, i, kr, cs: (s, i, 0)),
        ),
        out_shape=jax.ShapeDtypeStruct((n_seq, seq_len, CHUNK), BF16),
        compiler_params=_cparams(2),
        name="mixer_a",
    )(jnp.asarray(kr0s), jnp.asarray(case_ids), main3, main3, main3, bias)


def _band_kernel(*refs, window, n_kv, rep, length, has_sink, want_lse):
    refs = list(refs)
    sink_ref = refs.pop(0) if has_sink else None
    q_ref, k_ref, v_ref = refs[:3]
    o_ref = refs[3]
    lse_ref = refs[4] if want_lse else None
    tq = 2 * window
    kw = tq + 2 * window
    i = pl.program_id(1)
    n_heads = n_kv * rep
    n_sb = q_ref.shape[1] // tq

    def slab(sb):
        q0 = i * q_ref.shape[1] + sb * tq
        ks = pl.multiple_of(jnp.clip(q0 - window, 0, length - kw), 64)
        return q0, ks

    def scores(sb, h):
        _, ks = slab(sb)
        g = h // rep
        q = q_ref[0, sb * tq:(sb + 1) * tq, h * HEAD_DIM:(h + 1) * HEAD_DIM]
        k = k_ref[0, pl.ds(ks, kw), g * HEAD_DIM:(g + 1) * HEAD_DIM]
        return lax.dot_general(q, k, (((1,), (1,)), ((), ())), preferred_element_type=F32)

    chains = [(sb, h) for sb in range(n_sb) for h in range(n_heads)]
    s_next = scores(*chains[0])
    outs, lses = [], []
    for idx, (sb, h) in enumerate(chains):
        s = s_next
        if idx + 1 < len(chains):
            s_next = scores(*chains[idx + 1])
        q0, ks = slab(sb)
        g = h // rep
        qpos = q0 + lax.broadcasted_iota(I32, (tq, kw), 0)
        kpos = ks + lax.broadcasted_iota(I32, (tq, kw), 1)
        s = jnp.where(jnp.abs(qpos - kpos) <= window, s, NEG_INF)
        m = jnp.max(s, axis=-1, keepdims=True)
        if has_sink:
            m = jnp.maximum(m, sink_ref[h])
        e = jnp.exp(s - m)
        den = jnp.sum(e, axis=-1, keepdims=True)
        if has_sink:
            den = den + jnp.exp(sink_ref[h] - m)
        v = v_ref[0, pl.ds(ks, kw), g * HEAD_DIM:(g + 1) * HEAD_DIM]
        o = jnp.dot(e.astype(BF16), v, preferred_element_type=F32)
        outs.append(o / den)
        if want_lse:
            lses.append(jnp.broadcast_to(m + jnp.log(den), (tq, HEAD_DIM)))
        if h == n_heads - 1:
            rows = slice(sb * tq, (sb + 1) * tq)
            o_ref[0, rows, :] = jnp.concatenate(outs, axis=-1).astype(BF16)
            if want_lse:
                lse_ref[0, rows, :] = jnp.concatenate(lses, axis=-1)
            outs, lses = [], []


def _band_attention(src3, q_blk, k_blk, v_blk, kv_width, window, n_kv, rep, sink=None,
                    want_lse=False):
    n_seq, length, _ = src3.shape
    tq = min(BAND_STEP_ROWS, length)
    qw = n_kv * rep * HEAD_DIM
    kern = functools.partial(_band_kernel, window=window, n_kv=n_kv, rep=rep, length=length,
                             has_sink=sink is not None, want_lse=want_lse)
    in_specs = [
        pl.BlockSpec((1, tq, qw), lambda s, i: (s, i, q_blk)),
        pl.BlockSpec((1, length, kv_width), lambda s, i: (s, 0, k_blk)),
        pl.BlockSpec((1, length, kv_width), lambda s, i: (s, 0, v_blk)),
    ]
    args = [src3, src3, src3]
    if sink is not None:
        in_specs = [pl.BlockSpec(memory_space=pltpu.SMEM)] + in_specs
        args = [sink.astype(F32)] + args
    o_spec = pl.BlockSpec((1, tq, qw), lambda s, i: (s, i, 0))
    out_specs = [o_spec]
    out_shape = [jax.ShapeDtypeStruct((n_seq, length, qw), BF16)]
    if want_lse:
        out_specs.append(o_spec)
        out_shape.append(jax.ShapeDtypeStruct((n_seq, length, qw), F32))
    res = pl.pallas_call(
        kern,
        grid=(n_seq, length // tq),
        in_specs=in_specs,
        out_specs=out_specs,
        out_shape=out_shape,
        compiler_params=_cparams(2),
        name="mixer_band",
    )(*args)
    return res


def _diff_kernel(lam_ref, q_ref, kt_ref, vx_ref, g_ref, o_ref, q_scr, m_scr, acc_scr, *, seq_len):
    tk = min(DIFF_TK, seq_len)
    n_sub = 2 * DIFF_HEADS
    q = q_ref[0]
    for j in range(n_sub):
        q_scr[j] = q[:, j * DIFF_QK_DIM:(j + 1) * DIFF_QK_DIM]
    acc_scr[...] = jnp.zeros(acc_scr.shape, F32)
    lam = lam_ref[0]
    post = lam_ref[1]
    bound = lam_ref[2]

    def operands(c, j):
        c0 = pl.multiple_of(c * tk, tk)
        kt = kt_ref[0, j * DIFF_QK_DIM:(j + 1) * DIFF_QK_DIM, pl.ds(c0, tk)]
        vv = vx_ref[0, pl.ds(c0, tk), (j // 2) * V7X_LANES:(j // 2 + 1) * V7X_LANES]
        return kt, vv

    @pl.when(bound <= DIFF_FAST_BOUND)
    def _():
        def body(c, carry):
            for j in range(n_sub):
                kt, vv = operands(c, j)
                s = jnp.dot(q_scr[j], kt, preferred_element_type=F32)
                p = jnp.exp2(s - bound).astype(BF16)
                acc_scr[j] += jnp.dot(p, vv, preferred_element_type=F32)
            return carry

        lax.fori_loop(0, seq_len // tk, body, 0)

    @pl.when(bound > DIFF_FAST_BOUND)
    def _():
        m_scr[...] = jnp.full(m_scr.shape, NEG_INF, F32)

        def body(c, carry):
            for j in range(n_sub):
                kt, vv = operands(c, j)
                s = jnp.dot(q_scr[j], kt, preferred_element_type=F32)
                mi = m_scr[j]
                mn = jnp.maximum(mi, jnp.max(s, axis=-1, keepdims=True))
                p = jnp.exp2(s - mn)
                alpha = jnp.exp2(mi - mn)
                acc_scr[j] = alpha * acc_scr[j] + jnp.dot(p.astype(BF16), vv,
                                                          preferred_element_type=F32)
                m_scr[j] = mn
            return carry

        lax.fori_loop(0, seq_len // tk, body, 0)

    outs = []
    for h in range(DIFF_HEADS):
        a0 = acc_scr[2 * h]
        a1 = acc_scr[2 * h + 1]
        od = a0[:, :HEAD_DIM] / a0[:, HEAD_DIM:] - lam * (a1[:, :HEAD_DIM] / a1[:, HEAD_DIM:])
        ms = jnp.mean(od * od, axis=-1, keepdims=True)
        outs.append(od * lax.rsqrt(ms + NORM_EPS) * g_ref[...] * post)
    o_ref[0] = jnp.concatenate(outs, axis=-1).astype(BF16)


def _diff_attention(main3, kdt, vdx3, lam_post, subln):
    n_seq, seq_len, _ = main3.shape
    tq = DIFF_TQ
    kern = functools.partial(_diff_kernel, seq_len=seq_len)
    return pl.pallas_call(
        kern,
        grid=(n_seq, seq_len // tq),
        in_specs=[
            pl.BlockSpec(memory_space=pltpu.SMEM),
            pl.BlockSpec((1, tq, CHUNK), lambda s, i: (s, i, MAIN_COLS // CHUNK - 1)),
            pl.BlockSpec((1, CHUNK, seq_len), lambda s, i: (s, 0, 0),
                         pipeline_mode=pl.Buffered(1)),
            pl.BlockSpec((1, seq_len, 2 * CHUNK), lambda s, i: (s, 0, 0),
                         pipeline_mode=pl.Buffered(1)),
            pl.BlockSpec((1, HEAD_DIM), lambda s, i: (0, 0)),
        ],
        out_specs=pl.BlockSpec((1, tq, CHUNK), lambda s, i: (s, i, 0)),
        out_shape=jax.ShapeDtypeStruct((n_seq, seq_len, CHUNK), BF16),
        scratch_shapes=[
            pltpu.VMEM((2 * DIFF_HEADS, tq, DIFF_QK_DIM), BF16),
            pltpu.VMEM((2 * DIFF_HEADS, tq, 1), F32),
            pltpu.VMEM((2 * DIFF_HEADS, tq, V7X_LANES), F32),
        ],
        compiler_params=_cparams(2),
        name="mixer_d",
    )(lam_post, main3, kdt, vdx3, subln)


def _out_proj_kernel(h_ref, oa_ref, ob_ref, oc0_ref, oc1_ref, oc2_ref, l0_ref, l1_ref, l2_ref,
                     od_ref, w_ref, fn_ref, wr_ref, h1_ref, hn_ref, aff_ref, *scratch):
    tm = h_ref.shape[0]

    def token_major(ref, scr):
        dil = ref.shape[1]
        halves = []
        for half_i in range(CHUNK // V7X_LANES):
            lanes = slice(half_i * V7X_LANES, (half_i + 1) * V7X_LANES)
            for r in range(dil):
                scr.at[half_i][pl.ds(r, tm // dil, stride=dil), :] = ref[0, r, :, lanes].astype(F32)
            halves.append(scr[half_i])
        return jnp.concatenate(halves, axis=-1)

    l0 = l0_ref[...]
    l1 = token_major(l1_ref, scratch[0])
    l2 = token_major(l2_ref, scratch[1])
    mx = jnp.maximum(jnp.maximum(l0, l1), l2)
    e0, e1, e2 = jnp.exp(l0 - mx), jnp.exp(l1 - mx), jnp.exp(l2 - mx)
    den = e0 + e1 + e2
    oc = ((e0 / den) * oc0_ref[...].astype(F32) + (e1 / den) * token_major(oc1_ref, scratch[2])
          + (e2 / den) * token_major(oc2_ref, scratch[3]))
    acc = h_ref[...]
    acc = acc + jnp.dot(oa_ref[...], w_ref[0 * CHUNK:1 * CHUNK, :], preferred_element_type=F32)
    acc = acc + jnp.dot(ob_ref[...], w_ref[1 * CHUNK:2 * CHUNK, :], preferred_element_type=F32)
    acc = acc + jnp.dot(oc.astype(BF16), w_ref[2 * CHUNK:3 * CHUNK, :], preferred_element_type=F32)
    acc = acc + jnp.dot(od_ref[...], w_ref[3 * CHUNK:4 * CHUNK, :], preferred_element_type=F32)
    h1_ref[...] = acc
    ms = jnp.mean(acc * acc, axis=-1, keepdims=True)
    hn = (acc * lax.rsqrt(ms + NORM_EPS) * fn_ref[...]).astype(BF16)
    hn_ref[...] = hn
    logits = jnp.dot(hn, wr_ref[...], preferred_element_type=F32)
    m = jnp.max(logits, axis=-1, keepdims=True)
    e = jnp.exp(logits - m)
    aff_ref[...] = e / jnp.sum(e, axis=-1, keepdims=True)


def _out_proj(h2, oa, ob, ocs, lses, od, w_out, ffn_norm, w_router, tm, seq_len):
    n = h2.shape[0]
    nblk = seq_len // tm
    row = lambda w: pl.BlockSpec((tm, w), lambda i: (i, 0))
    const = lambda r, c: pl.BlockSpec((r, c), lambda i: (0, 0))
    dil = lambda g: _dilated_spec(DIL_CONFIGS[g][1], tm, CHUNK, nblk)
    return pl.pallas_call(
        _out_proj_kernel,
        grid=(n // tm,),
        in_specs=[row(D_MODEL), row(CHUNK), row(CHUNK), row(CHUNK), dil(1), dil(2),
                  row(CHUNK), dil(1), dil(2), row(CHUNK),
                  const(D_MODEL, D_MODEL), const(1, D_MODEL), const(D_MODEL, N_EXPERTS)],
        out_specs=[row(D_MODEL), row(D_MODEL), row(N_EXPERTS)],
        out_shape=[
            jax.ShapeDtypeStruct((n, D_MODEL), F32),
            jax.ShapeDtypeStruct((n, D_MODEL), BF16),
            jax.ShapeDtypeStruct((n, N_EXPERTS), F32),
        ],
        scratch_shapes=[pltpu.VMEM((CHUNK // V7X_LANES, tm, V7X_LANES), F32)] * 4,
        compiler_params=_cparams(1),
        name="out_proj",
    )(h2, oa, ob, ocs[0], ocs[1], ocs[2], lses[0], lses[1], lses[2], od, w_out, ffn_norm,
      w_router)


def _select_kernel(aff_ref, posm_ref, pref_ref, *, cap, n_tok):
    def bits():
        return lax.bitcast_convert_type(aff_ref[...], I32)

    def bit_body(i, thr):
        cand = thr | jnp.left_shift(jnp.int32(1), 30 - i)
        cnt = jnp.sum(jnp.where(bits() >= cand, 1.0, 0.0), axis=-1, keepdims=True)
        return jnp.where(cnt >= cap, cand, thr)

    thr = lax.fori_loop(0, 31, bit_body, jnp.zeros((N_EXPERTS, 1), I32))
    cnt_gt = jnp.sum(jnp.where(bits() > thr, 1.0, 0.0), axis=-1, keepdims=True)
    need = cap - cnt_gt
    blk = SLOT_BLK
    r = lax.broadcasted_iota(I32, (blk, blk), 0)
    c = lax.broadcasted_iota(I32, (blk, blk), 1)
    upper = jnp.where(r < c, 1.0, 0.0).astype(BF16)

    def blk_body(b, carry):
        ceq, csel = carry
        b0 = pl.multiple_of(b * blk, blk)
        xb = lax.bitcast_convert_type(aff_ref[:, pl.ds(b0, blk)], I32)
        gt = xb > thr
        eq = xb == thr
        eqf = jnp.where(eq, 1.0, 0.0)
        rank = ceq + jnp.dot(eqf.astype(BF16), upper, preferred_element_type=F32)
        sel = gt | (eq & (rank < need))
        self_ = jnp.where(sel, 1.0, 0.0)
        pref = csel + jnp.dot(self_.astype(BF16), upper, preferred_element_type=F32)
        prefi = pref.astype(I32)
        pref_ref[:, pl.ds(b0, blk)] = prefi
        posm_ref[:, pl.ds(b0, blk)] = jnp.where(sel, prefi, -1)
        return (ceq + jnp.sum(eqf, axis=-1, keepdims=True),
                csel + jnp.sum(self_, axis=-1, keepdims=True))

    zero = jnp.zeros((N_EXPERTS, 1), F32)
    lax.fori_loop(0, n_tok // blk, blk_body, (zero, zero))


def _select(aff_t, cap):
    n_tok = aff_t.shape[1]
    kern = functools.partial(_select_kernel, cap=cap, n_tok=n_tok)
    full = pl.BlockSpec((N_EXPERTS, n_tok), lambda i: (0, 0))
    return pl.pallas_call(
        kern,
        grid=(1,),
        in_specs=[full],
        out_specs=[full, full],
        out_shape=[jax.ShapeDtypeStruct((N_EXPERTS, n_tok), I32)] * 2,
        compiler_params=_cparams(1),
        name="select",
    )(aff_t)


_VALID, _FIRST, _LAST = 1, 2, 4


_NO_HIT = 1 << 30


def _block_windows(pref, cap, tok_blk, slot_blk):
    n_exp = pref.shape[0]
    nw = cap // slot_blk
    off = pref[:, ::tok_blk]
    end = jnp.concatenate([off[:, 1:], jnp.full((n_exp, 1), cap, I32)], axis=1)
    cnt = end - off
    w_lo = jnp.minimum(off // slot_blk, nw - 1)
    w_hi = jnp.where(cnt > 0, (end - 1) // slot_blk, w_lo)
    return cnt, w_lo, w_hi


def _search(starts, idx):
    return (jnp.sum(starts[None, :] <= idx.reshape(-1, 1), axis=1) - 1).astype(I32).reshape(idx.shape)


def _gather_plan(pref, cap):
    n_exp, n_tok = pref.shape
    nb = n_tok // GATHER_TOK_BLK
    nw = cap // SLOT_BLK
    _, w_lo, w_hi = _block_windows(pref, cap, GATHER_TOK_BLK, SLOT_BLK)
    s_g = nb + nw
    nst = w_hi - w_lo + 1
    start = jnp.cumsum(nst, axis=1) - nst
    total = jnp.sum(nst, axis=1, keepdims=True)
    s_idx = jnp.arange(s_g, dtype=I32)[None, :]
    s_eff = jnp.minimum(s_idx, total - 1)
    b_s = jax.vmap(_search)(start, s_eff)
    w_s = jnp.take_along_axis(w_lo, b_s, axis=1) + (s_eff - jnp.take_along_axis(start, b_s, axis=1))
    valid = s_idx < total
    prev_w = jnp.concatenate([jnp.full((n_exp, 1), -1, I32), w_s[:, :-1]], axis=1)
    next_w = jnp.concatenate([w_s[:, 1:], jnp.full((n_exp, 1), -1, I32)], axis=1)
    first = valid & (w_s != prev_w)
    last = valid & ((w_s != next_w) | (s_idx == total - 1))
    flag = valid * _VALID + first * _FIRST + last * _LAST
    return (b_s.reshape(-1), w_s.reshape(-1).astype(I32), flag.reshape(-1).astype(I32)), s_g


def _combine_plan(pref, cap):
    n_exp, n_tok = pref.shape
    nb = n_tok // COMBINE_TOK_BLK
    nw = cap // COMBINE_SLOT_BLK
    grp = COMBINE_GROUP
    cnt, w_lo, w_hi = _block_windows(pref, cap, COMBINE_TOK_BLK, COMBINE_SLOT_BLK)
    cnt_t, wlo_t, whi_t = cnt.T, w_lo.T, w_hi.T
    n_win = jnp.where(cnt_t > 0, whi_t - wlo_t + 1, 0)
    win_start = (jnp.cumsum(n_win.reshape(-1)) - n_win.reshape(-1)).astype(I32)
    per_blk = jnp.sum(n_win, axis=1)
    blk_first_win = (jnp.cumsum(per_blk) - per_blk).astype(I32)
    steps = jnp.maximum((per_blk + grp - 1) // grp, 1)
    step_start = (jnp.cumsum(steps) - steps).astype(I32)
    total = jnp.sum(steps)
    s_c = (n_exp * (nb + nw) + grp - 1) // grp + nb
    s_idx = jnp.arange(s_c, dtype=I32)
    s_eff = jnp.minimum(s_idx, total - 1)
    cb = _search(step_start, s_eff)
    local = s_eff - step_start[cb]
    q = local[:, None] * grp + jnp.arange(grp, dtype=I32)[None, :]
    live = (q < per_blk[cb][:, None]) & (s_idx < total)[:, None]
    win = jnp.where(live, blk_first_win[cb][:, None] + q, 0)
    pair = _search(win_start, win.reshape(-1)).reshape(win.shape)
    ce = pair % n_exp
    cw = wlo_t.reshape(-1)[pair] + (win - win_start[pair])
    cw = jnp.clip(cw, 0, nw - 1)
    blk = (ce * nw + cw).astype(I32)
    base = jnp.where(live, cw * COMBINE_SLOT_BLK, _NO_HIT).astype(I32)
    valid = s_idx < total
    prev_b = jnp.concatenate([jnp.full((1,), -1, I32), cb[:-1]])
    next_b = jnp.concatenate([cb[1:], jnp.full((1,), -1, I32)])
    first = valid & (cb != prev_b)
    last = valid & ((cb != next_b) | (s_idx == total - 1))
    flag = (valid * _VALID + first * _FIRST + last * _LAST).astype(I32)
    return (cb, ce.reshape(-1).astype(I32), blk.reshape(-1), base.reshape(-1), flag), s_c


def _gather_kernel(gb_ref, gw_ref, gf_ref, posm_ref, g_ref, hn_ref, xe_ref, gs_ref, acc_ref,
                   gacc_ref, *, n_steps):
    e = pl.program_id(0)
    s = pl.program_id(1)
    flag = gf_ref[e * n_steps + s]
    w = gw_ref[e * n_steps + s]

    @pl.when((flag & _FIRST) != 0)
    def _():
        acc_ref[...] = jnp.zeros_like(acc_ref)
        gacc_ref[...] = jnp.zeros_like(gacc_ref)

    @pl.when((flag & _VALID) != 0)
    def _():
        rel = posm_ref[0] - w * SLOT_BLK
        hit = lax.broadcasted_iota(I32, (SLOT_BLK, GATHER_TOK_BLK), 0) == rel
        onehot = jnp.where(hit, 1.0, 0.0).astype(BF16)
        acc_ref[...] += jnp.dot(onehot, hn_ref[...], preferred_element_type=F32)
        gacc_ref[...] += jnp.sum(jnp.where(hit, g_ref[0], 0.0), axis=-1, keepdims=True)

    @pl.when((flag & _LAST) != 0)
    def _():
        xe_ref[0] = acc_ref[...].astype(BF16)
        gs_ref[0] = gacc_ref[...]


def _gather(plan, s_g, posm3, g3, hn, cap):
    gb, gw, gf = plan
    n_tok = hn.shape[0]
    kern = functools.partial(_gather_kernel, n_steps=s_g)
    tok = lambda e, s, gb, gw, gf: (e, 0, gb[e * s_g + s])
    slot = lambda e, s, gb, gw, gf: (e, gw[e * s_g + s], 0)
    return pl.pallas_call(
        kern,
        grid_spec=pltpu.PrefetchScalarGridSpec(
            num_scalar_prefetch=3,
            grid=(N_EXPERTS, s_g),
            in_specs=[
                pl.BlockSpec((1, 1, GATHER_TOK_BLK), tok),
                pl.BlockSpec((1, 1, GATHER_TOK_BLK), tok),
                pl.BlockSpec((GATHER_TOK_BLK, D_MODEL), lambda e, s, gb, gw, gf: (gb[e * s_g + s], 0)),
            ],
            out_specs=[
                pl.BlockSpec((1, SLOT_BLK, D_MODEL), slot),
                pl.BlockSpec((1, SLOT_BLK, 1), slot),
            ],
            scratch_shapes=[pltpu.VMEM((SLOT_BLK, D_MODEL), F32), pltpu.VMEM((SLOT_BLK, 1), F32)],
        ),
        out_shape=[
            jax.ShapeDtypeStruct((N_EXPERTS, cap, D_MODEL), BF16),
            jax.ShapeDtypeStruct((N_EXPERTS, cap, 1), F32),
        ],
        compiler_params=_cparams(2),
        name="gather",
    )(gb, gw, gf, posm3, g3, hn)


def _experts_kernel(layer_ref, x_ref, gs_ref, wg_ref, wu_ref, wd_ref, y_ref):
    del layer_ref
    x = x_ref[0]
    hg = jnp.dot(x, wg_ref[0, 0], preferred_element_type=F32)
    hu = jnp.dot(x, wu_ref[0, 0], preferred_element_type=F32)
    hdn = (hg / (1.0 + jnp.exp(-hg)) * hu).astype(BF16)
    y = jnp.dot(hdn, wd_ref[0, 0], preferred_element_type=F32)
    y_ref[0] = (y * gs_ref[0]).astype(BF16)


def _experts(layer, xe, gs, w_gate, w_up, w_down):
    n_exp, cap, _ = xe.shape
    tc = min(512, cap)
    wspec = pl.BlockSpec((1, 1, D_MODEL, D_MODEL), lambda e, c, layer: (layer[0], e, 0, 0))
    return pl.pallas_call(
        _experts_kernel,
        grid_spec=pltpu.PrefetchScalarGridSpec(
            num_scalar_prefetch=1,
            grid=(n_exp, cap // tc),
            in_specs=[
                pl.BlockSpec((1, tc, D_MODEL), lambda e, c, layer: (e, c, 0)),
                pl.BlockSpec((1, tc, 1), lambda e, c, layer: (e, c, 0)),
                wspec, wspec, wspec,
            ],
            out_specs=pl.BlockSpec((1, tc, D_MODEL), lambda e, c, layer: (e, c, 0)),
        ),
        out_shape=jax.ShapeDtypeStruct((n_exp, cap, D_MODEL), BF16),
        compiler_params=_cparams(2),
        name="experts",
    )(layer.reshape(1).astype(I32), xe, gs, w_gate, w_up, w_down)


def _combine_kernel(cb_ref, ce_ref, cblk_ref, cbase_ref, cf_ref, h1_ref, posmt_ref, *rest):
    del cb_ref, cblk_ref
    grp = COMBINE_GROUP
    ye_refs = rest[:grp]
    p_ref, pn_ref, wpg_ref, wpp_ref, o_ref, acc_ref = rest[grp:]
    s = pl.program_id(0)
    flag = cf_ref[s]

    @pl.when((flag & _FIRST) != 0)
    def _():
        acc_ref[...] = jnp.zeros_like(acc_ref)

    @pl.when((flag & _VALID) != 0)
    def _():
        pm = posmt_ref[...].astype(F32)
        lane = lax.broadcasted_iota(I32, pm.shape, 1)
        slot_iota = lax.broadcasted_iota(I32, (COMBINE_TOK_BLK, COMBINE_SLOT_BLK), 1)
        hots = []
        for k in range(grp):
            e = ce_ref[s * grp + k]
            col = jnp.sum(jnp.where(lane == e, pm, 0.0), axis=-1, keepdims=True)
            rel = col.astype(I32) - cbase_ref[s * grp + k]
            hots.append(jnp.where(slot_iota == rel, 1.0, 0.0).astype(BF16))
        onehot = jnp.concatenate(hots, axis=1)
        rows = jnp.concatenate([r[0] for r in ye_refs], axis=0)
        acc_ref[...] += jnp.dot(onehot, rows, preferred_element_type=F32)

    @pl.when((flag & _LAST) != 0)
    def _():
        h2 = h1_ref[...] + acc_ref[...]
        ms = jnp.mean(h2 * h2, axis=-1, keepdims=True)
        hn = (h2 * lax.rsqrt(ms + NORM_EPS) * pn_ref[...]).astype(BF16)
        z = jnp.dot(hn, wpg_ref[...], preferred_element_type=F32)
        gate = 1.0 / (1.0 + jnp.exp(-z))
        proj = jnp.dot(p_ref[...].astype(BF16), wpp_ref[...], preferred_element_type=F32)
        o_ref[...] = h2 + gate * proj


def _combine(plan, s_c, h1, posm_t, ye, p2, ple_norm, w_ple_gate, w_ple_proj, cap):
    cb, ce, cblk, cbase, cf = plan
    n_tok = h1.shape[0]
    nw = cap // COMBINE_SLOT_BLK
    grp = COMBINE_GROUP
    ye3 = ye.reshape(N_EXPERTS * nw, COMBINE_SLOT_BLK, D_MODEL)
    tokb = lambda width: pl.BlockSpec((COMBINE_TOK_BLK, width), lambda s, cb, *_: (cb[s], 0))
    const = lambda r, c: pl.BlockSpec((r, c), lambda s, *_: (0, 0))

    def ye_spec(k):
        return pl.BlockSpec((1, COMBINE_SLOT_BLK, D_MODEL),
                            lambda s, cb, ce, cblk, *_: (cblk[s * grp + k], 0, 0))

    return pl.pallas_call(
        _combine_kernel,
        grid_spec=pltpu.PrefetchScalarGridSpec(
            num_scalar_prefetch=5,
            grid=(s_c,),
            in_specs=[tokb(D_MODEL), tokb(N_EXPERTS)] + [ye_spec(k) for k in range(grp)] + [
                tokb(PLE_DIM), const(1, D_MODEL), const(D_MODEL, D_MODEL),
                const(PLE_DIM, D_MODEL)],
            out_specs=tokb(D_MODEL),
            scratch_shapes=[pltpu.VMEM((COMBINE_TOK_BLK, D_MODEL), F32)],
        ),
        out_shape=jax.ShapeDtypeStruct((n_tok, D_MODEL), F32),
        compiler_params=_cparams(1),
        name="combine",
    )(cb, ce, cblk, cbase, cf, h1, posm_t, *([ye3] * grp), p2, ple_norm, w_ple_gate, w_ple_proj)


def _layer(h2, p2, lw, static):
    n_seq, seq_len, tabs, b64, b32, na_plan = static
    n_tok = h2.shape[0]
    tm = min(512, seq_len)
    gains = _in_gains(lw["qk_norm_a"], lw["qk_norm_b"], lw["qk_norm_c"], lw["qk_norm_d"])
    main, cg1, cg2, kdt, vdx = _in_proj(h2, lw["attn_norm"][None, :], lw["w_in"], b64, b32, gains,
                                        tabs, n_seq, seq_len, min(IN_PROJ_ROWS, seq_len))
    main3 = main.reshape(n_seq, seq_len, MAIN_COLS)

    kr0s, case_ids = na_plan
    oa = _na_attention(main3, lw["na_bias"], kr0s, case_ids)

    (ob,) = _band_attention(main3, 3, 8, 9, V7X_LANES, SW_WINDOW, SW_KV_HEADS,
                            SW_HEADS // SW_KV_HEADS, sink=lw["sink_b"])
    ocs, lses = [], []
    for g_i, (win, dil) in enumerate(DIL_CONFIGS):
        if dil == 1:
            src, cols = main3, (5, 6, 7)
        else:
            src = (cg1, cg2)[g_i - 1].reshape(n_seq * dil, seq_len // dil, 3 * CHUNK)
            cols = (0, 1, 2)
        o_g, l_g = _band_attention(src, *cols, CHUNK, win // (2 * dil), DIL_HEADS, 1,
                                   want_lse=True)
        shape = (n_tok, CHUNK) if dil == 1 else (n_seq, dil, seq_len // dil, CHUNK)
        ocs.append(o_g.reshape(shape))
        lses.append(l_g.reshape(shape))

    lf = lw["lambda_d"].astype(F32)
    lam = jnp.exp(jnp.sum(lf[0] * lf[1])) - jnp.exp(jnp.sum(lf[2] * lf[3])) + lw["lam_init"]
    gd = jnp.abs(lw["qk_norm_d"].astype(F32))
    score_bound = (LOG2_E * DIFF_QK_DIM ** 0.5 * (1.0 + 2.0 ** -6)) * jnp.max(gd[0]) * jnp.max(gd[1])
    lam_post = jnp.stack([lam, 1.0 - lw["lam_init"], score_bound]).astype(F32)
    od = _diff_attention(main3, kdt, vdx.reshape(n_seq, seq_len, 2 * CHUNK), lam_post,
                         lw["subln_d"].astype(F32)[None, :])

    h1, hn, aff = _out_proj(h2, oa.reshape(n_tok, CHUNK), ob.reshape(n_tok, CHUNK), ocs, lses,
                            od.reshape(n_tok, CHUNK), lw["w_out"], lw["ffn_norm"][None, :],
                            lw["w_router"], tm, seq_len)

    cap = EC_CAPACITY_FACTOR * n_tok // N_EXPERTS
    aff_t = aff.T
    posm, pref = _select(aff_t, cap)
    gather_plan, s_g = _gather_plan(pref, cap)
    combine_plan, s_c = _combine_plan(pref, cap)
    xe, gs = _gather(gather_plan, s_g, posm.reshape(N_EXPERTS, 1, n_tok),
                     aff_t.reshape(N_EXPERTS, 1, n_tok), hn, cap)
    ye = _experts(lw["layer"], xe, gs, lw["w_gate"], lw["w_up"], lw["w_down"])
    return _combine(combine_plan, s_c, h1, posm.T, ye, p2, lw["ple_norm"][None, :],
                    lw["w_ple_gate"], lw["w_ple_proj"], cap)


def _trunk(x, p, weights):
    n_seq, seq_len, _ = x.shape
    depth = p.shape[0]
    tabs = _rope_tables(seq_len, HEAD_DIM // 2, 4) + _rope_tables(seq_len, DIFF_QK_DIM // 2, 8)
    kr0s, case_ids, mask, ir, ic = _na_plan(seq_len // GRID_W)
    static = (n_seq, seq_len, tabs, _block_mean_matrix(HEAD_DIM), _block_mean_matrix(DIFF_QK_DIM),
              (kr0s, case_ids))
    stacked = ("w_gate", "w_up", "w_down")
    xs = {k: v for k, v in weights.items() if k not in stacked}
    xs["p"] = p.reshape(depth, n_seq * seq_len, PLE_DIM)
    xs["na_bias"] = _na_bias(weights["rpb_a"], mask, ir, ic)
    xs["layer"] = jnp.arange(depth, dtype=I32)

    def step(h2, lw):
        lw = dict(lw, **{k: weights[k] for k in stacked})
        return _layer(h2, lw["p"], lw, static), None

    h2, _ = lax.scan(step, x.reshape(n_seq * seq_len, D_MODEL), xs)
    return h2.reshape(n_seq, seq_len, D_MODEL)


def kernel(x_prompt, x_sample, p_prompt, p_sample, attn_norm, w_in, rpb_a, qk_norm_a, qk_norm_b, sink_b, qk_norm_c, qk_norm_d, lambda_d, subln_d, w_out, ffn_norm, w_router, w_gate, w_up, w_down, ple_norm, w_ple_gate, w_ple_proj):
    depth = w_in.shape[0]
    lam_init = jnp.asarray([0.8 - 0.6 * math.exp(-0.3 * i) for i in range(depth)], F32)
    weights = dict(
        attn_norm=attn_norm.astype(F32), w_in=jax.vmap(_extend_w_in)(w_in), rpb_a=rpb_a,
        qk_norm_a=qk_norm_a, qk_norm_b=qk_norm_b, sink_b=sink_b, qk_norm_c=qk_norm_c,
        qk_norm_d=qk_norm_d, lambda_d=lambda_d, subln_d=subln_d, w_out=w_out.astype(BF16),
        ffn_norm=ffn_norm.astype(F32), w_router=w_router.astype(BF16), w_gate=w_gate.astype(BF16),
        w_up=w_up.astype(BF16), w_down=w_down.astype(BF16), ple_norm=ple_norm.astype(F32),
        w_ple_gate=w_ple_gate.astype(BF16), w_ple_proj=w_ple_proj.astype(BF16), lam_init=lam_init)
    y_prompt = _trunk(x_prompt, p_prompt, weights)
    y_sample = _trunk(x_sample, p_sample, weights)
    return (y_prompt, y_sample)
```
